```python
import math
import jax, jax.numpy as jnp
from jax import lax
import numpy as np


D_MODEL = 1024
BATCH = 2
SEQ = 16384
DEPTH = 2

PLE_DIM = 256
N_DIFF_HEADS = 4
DIFF_QK_DIM = 64
DIFF_V_DIM = 2 * DIFF_QK_DIM
ATTN_WIDTH = N_DIFF_HEADS * DIFF_V_DIM
CONV_WIDTH = D_MODEL - ATTN_WIDTH
CONV_KERNEL = 31
ROT_DIM = DIFF_QK_DIM // 4
ROPE_THETA = 500000.0
D_FF = 2752
FFN_CONV_KERNEL = 3
Q_BLOCK = 128
NORM_EPS = 1e-6
MASK_VALUE = -1e30
Q_COLS = N_DIFF_HEADS * 2 * DIFF_QK_DIM
IN_COLS = 2 * Q_COLS + ATTN_WIDTH + 2 * CONV_WIDTH

kernel_name = 'hybrid_diffattn_conformerconv_convffn_ple'


def rms_norm(x, g):
    xf = x.astype(jnp.float32)
    y = xf * lax.rsqrt(jnp.mean(xf * xf, axis=-1, keepdims=True) + NORM_EPS)
    return (y * g.astype(jnp.float32)).astype(x.dtype)


def layer_norm(x, g, b):
    xf = x.astype(jnp.float32)
    mu = jnp.mean(xf, axis=-1, keepdims=True)
    xc = xf - mu
    y = xc * lax.rsqrt(jnp.mean(xc * xc, axis=-1, keepdims=True) + NORM_EPS)
    return (y * g.astype(jnp.float32) + b.astype(jnp.float32)).astype(x.dtype)


def causal_dwconv(x, w, b):
    K, C = w.shape
    y = lax.conv_general_dilated(x, w[:, None, :].astype(x.dtype), window_strides=(1,),
                                 padding=[(K - 1, 0)],
                                 dimension_numbers=('NWC', 'WIO', 'NWC'),
                                 feature_group_count=C)
    return y + b.astype(x.dtype)


def rope_tables(positions):
    inv_freq = ROPE_THETA ** (-jnp.arange(0, ROT_DIM, 2, dtype=jnp.float32) / ROT_DIM)
    ang = positions.astype(jnp.float32)[..., None] * inv_freq
    return jnp.cos(ang)[:, :, None, None, :], jnp.sin(ang)[:, :, None, None, :]


def apply_partial_rope(x, cos, sin):
    xr = x[..., :ROT_DIM].astype(jnp.float32)
    x1, x2 = xr[..., :ROT_DIM // 2], xr[..., ROT_DIM // 2:]
    rot = jnp.concatenate([x1 * cos - x2 * sin, x2 * cos + x1 * sin], axis=-1)
    return jnp.concatenate([rot.astype(x.dtype), x[..., ROT_DIM:]], axis=-1)


def diff_attention(q, k, v, lam):
    B, S = q.shape[0], q.shape[1]
    nb = S // Q_BLOCK
    scale = DIFF_QK_DIM ** -0.5
    qf = jnp.transpose(q.astype(jnp.float32), (0, 2, 3, 1, 4)) * scale
    kf = jnp.transpose(k.astype(jnp.float32), (0, 2, 3, 1, 4))
    vf = jnp.transpose(v.astype(jnp.float32), (0, 2, 1, 3))
    qb = jnp.moveaxis(qf.reshape(B, N_DIFF_HEADS, 2, nb, Q_BLOCK, DIFF_QK_DIM), 3, 0)
    key_pos = jnp.arange(S)

    def one_block(args):
        q_blk, blk = args
        q_pos = blk * Q_BLOCK + jnp.arange(Q_BLOCK)
        s = jnp.einsum('bhmqd,bhmkd->bhmqk', q_blk, kf)
        s = jnp.where(key_pos[None, :] <= q_pos[:, None], s, MASK_VALUE)
        pr = jax.nn.softmax(s, axis=-1)
        a = pr[:, :, 0] - lam * pr[:, :, 1]
        return jnp.einsum('bhqk,bhkv->bhqv', a, vf)

    o = lax.map(one_block, (qb, jnp.arange(nb)))
    return jnp.transpose(o, (1, 0, 3, 2, 4)).reshape(B, S, N_DIFF_HEADS, DIFF_V_DIM)


def setup_inputs(seed: int = 0) -> dict:
    key = jax.random.key(seed)
    ks = jax.random.split(key, 32)

    def nrm(k, shape, scale):
        return jax.random.normal(k, shape, jnp.float32) * scale

    def gain(k, shape):
        return 1.0 + nrm(k, shape, 0.02)

    return {
        'x': nrm(ks[0], (BATCH, SEQ, D_MODEL), 1.0),
        'p': nrm(ks[1], (DEPTH, BATCH, SEQ, PLE_DIM), 1.0),
        'positions': jnp.broadcast_to(jnp.arange(SEQ, dtype=jnp.int32)[None, :], (BATCH, SEQ)),
        'norm1_g': gain(ks[2], (DEPTH, D_MODEL)),
        'w_in': nrm(ks[3], (DEPTH, D_MODEL, IN_COLS), D_MODEL ** -0.5),
        'q_norm_g': gain(ks[4], (DEPTH, DIFF_QK_DIM)),
        'k_norm_g': gain(ks[5], (DEPTH, DIFF_QK_DIM)),
        'lam_q1': nrm(ks[6], (DEPTH, DIFF_QK_DIM), 0.1),
        'lam_k1': nrm(ks[7], (DEPTH, DIFF_QK_DIM), 0.1),
        'lam_q2': nrm(ks[8], (DEPTH, DIFF_QK_DIM), 0.1),
        'lam_k2': nrm(ks[9], (DEPTH, DIFF_QK_DIM), 0.1),
        'subln_g': gain(ks[10], (DEPTH, DIFF_V_DIM)),
        'conv_w': nrm(ks[11], (DEPTH, CONV_KERNEL, CONV_WIDTH), CONV_KERNEL ** -0.5),
        'conv_b': nrm(ks[12], (DEPTH, CONV_WIDTH), 0.02),
        'conv_ln_g': gain(ks[13], (DEPTH, CONV_WIDTH)),
        'conv_ln_b': nrm(ks[14], (DEPTH, CONV_WIDTH), 0.02),
        'w_out': nrm(ks[15], (DEPTH, D_MODEL, D_MODEL), D_MODEL ** -0.5),
        'norm2_g': gain(ks[16], (DEPTH, D_MODEL)),
        'w_up': nrm(ks[17], (DEPTH, D_MODEL, 2 * D_FF), D_MODEL ** -0.5),
        'ffn_conv_w': nrm(ks[18], (DEPTH, FFN_CONV_KERNEL, 2 * D_FF), FFN_CONV_KERNEL ** -0.5),
        'ffn_conv_b': nrm(ks[19], (DEPTH, 2 * D_FF), 0.02),
        'w_down': nrm(ks[20], (DEPTH, D_FF, D_MODEL), D_FF ** -0.5),
        'ple_norm_g': gain(ks[21], (DEPTH, D_MODEL)),
        'w_ple_gate': nrm(ks[22], (DEPTH, D_MODEL, D_MODEL), D_MODEL ** -0.5),
        'b_ple_gate': nrm(ks[23], (DEPTH, D_MODEL), 0.02),
        'w_ple': nrm(ks[24], (DEPTH, PLE_DIM, D_MODEL), PLE_DIM ** -0.5),
    }


def reference(x, p, positions, norm1_g, w_in, q_norm_g, k_norm_g, lam_q1, lam_k1, lam_q2,
              lam_k2, subln_g, conv_w, conv_b, conv_ln_g, conv_ln_b, w_out, norm2_g, w_up,
              ffn_conv_w, ffn_conv_b, w_down, ple_norm_g, w_ple_gate, b_ple_gate, w_ple):
    B, S, _ = x.shape
    cos, sin = rope_tables(positions)
    h = x
    for i in range(DEPTH):
        hn = rms_norm(h, norm1_g[i])
        u = jnp.einsum('bsd,de->bse', hn, w_in[i])
        q = u[..., :Q_COLS].reshape(B, S, N_DIFF_HEADS, 2, DIFF_QK_DIM)
        k = u[..., Q_COLS:2 * Q_COLS].reshape(B, S, N_DIFF_HEADS, 2, DIFF_QK_DIM)
        v = u[..., 2 * Q_COLS:2 * Q_COLS + ATTN_WIDTH].reshape(B, S, N_DIFF_HEADS, DIFF_V_DIM)
        glu_in = u[..., 2 * Q_COLS + ATTN_WIDTH:]

        q = apply_partial_rope(rms_norm(q, q_norm_g[i]), cos, sin)
        k = apply_partial_rope(rms_norm(k, k_norm_g[i]), cos, sin)
        lam_init = 0.8 - 0.6 * math.exp(-0.3 * i)
        lam = (jnp.exp(jnp.sum(lam_q1[i].astype(jnp.float32) * lam_k1[i].astype(jnp.float32)))
               - jnp.exp(jnp.sum(lam_q2[i].astype(jnp.float32) * lam_k2[i].astype(jnp.float32)))
               + lam_init)
        o = diff_attention(q, k, v, lam)
        o = rms_norm(o, subln_g[i]) * (1.0 - lam_init)
        attn_out = o.reshape(B, S, ATTN_WIDTH).astype(h.dtype)

        ga, gb = glu_in[..., :CONV_WIDTH], glu_in[..., CONV_WIDTH:]
        c = ga * jax.nn.sigmoid(gb)
        c = causal_dwconv(c, conv_w[i], conv_b[i])
        c = jax.nn.silu(layer_norm(c, conv_ln_g[i], conv_ln_b[i]))

        mixed = jnp.concatenate([attn_out, c], axis=-1)
        h = h + jnp.einsum('bse,ed->bsd', mixed, w_out[i])

        hn = rms_norm(h, norm2_g[i])
        up = jnp.einsum('bsd,df->bsf', hn, w_up[i])
        up = causal_dwconv(up, ffn_conv_w[i], ffn_conv_b[i])
        act = jax.nn.silu(up[..., :D_FF]) * up[..., D_FF:]
        h = h + jnp.einsum('bsf,fd->bsd', act, w_down[i])

        gate = jax.nn.sigmoid(jnp.einsum('bsd,de->bse', rms_norm(h, ple_norm_g[i]), w_ple_gate[i])
                              + b_ple_gate[i])
        h = h + gate * jnp.einsum('bsp,pd->bsd', p[i], w_ple[i])
    return h
```

```python
import functools
import math

import jax
import jax.numpy as jnp
from jax import lax
from jax.experimental import pallas as pl
from jax.experimental.pallas import tpu as pltpu

N_HEADS = 4
QK_DIM = 64
V_DIM = 2 * QK_DIM
ATTN_WIDTH = N_HEADS * V_DIM
Q_COLS = N_HEADS * 2 * QK_DIM
ROT_DIM = QK_DIM // 4
HALF_ROT = ROT_DIM // 2
ROPE_THETA = 500000.0
NORM_EPS = 1e-6
MASK_VALUE = -1e30

LANES = 128
SUBLANES = 8
VMEM_LIMIT_BYTES = 56 * 1024 * 1024

ROW_TILE = 512
CONV_HALO = 32
FFN_HALO = 8

F32 = jnp.float32
BF16 = jnp.bfloat16


def _const_spec(shape):
    return pl.BlockSpec(shape, lambda *_: (0,) * len(shape), pipeline_mode=pl.Buffered(1))


def _rms_rows(x, g):
    ms = jnp.mean(x * x, axis=-1, keepdims=True)
    return x * lax.rsqrt(ms + NORM_EPS) * g


def _in_proj_kernel(h_ref, pos_ref, g1_ref, wqkv_t_ref, wglu_ref, qg_ref, kg_ref, invf_ref,
                    q_t_ref, k_ref, v_t_ref, c_ref, *, conv_width):
    hn = _rms_rows(h_ref[0], g1_ref[...]).astype(BF16)
    u_t = lax.dot_general(wqkv_t_ref[...], hn, (((1,), (1,)), ((), ())),
                          preferred_element_type=F32)
    glu = jnp.dot(hn, wglu_ref[...], preferred_element_type=F32)
    c_ref[0] = glu[:, :conv_width] * jax.nn.sigmoid(glu[:, conv_width:])

    ang = invf_ref[...] * pos_ref[0, 0]
    cos, sin = jnp.cos(ang), jnp.sin(ang)

    def norm_rope(x_t, g):
        segs = []
        for s in range(Q_COLS // QK_DIM):
            seg = x_t[s * QK_DIM:(s + 1) * QK_DIM]
            ms = jnp.mean(seg * seg, axis=0, keepdims=True)
            y = seg * lax.rsqrt(ms + NORM_EPS) * g
            x1, x2 = y[:HALF_ROT], y[HALF_ROT:ROT_DIM]
            segs += [x1 * cos - x2 * sin, x2 * cos + x1 * sin, y[ROT_DIM:]]
        return jnp.concatenate(segs, axis=0)

    q_t = norm_rope(u_t[:Q_COLS], qg_ref[...]) * (QK_DIM ** -0.5)
    k_t = norm_rope(u_t[Q_COLS:2 * Q_COLS], kg_ref[...])
    q_t_ref[0, 0] = q_t.astype(BF16)
    k_ref[0] = k_t.T.astype(BF16)
    v_t_ref[0, 0] = u_t[2 * Q_COLS:].astype(BF16)


def _in_proj(h, pos_f, g1, wqkv_t, wglu, qg, kg, invf):
    b, s, d = h.shape
    tm = ROW_TILE
    nt = s // tm
    conv_width = wglu.shape[1] // 2
    qkv_rows = wqkv_t.shape[0]
    return pl.pallas_call(
        functools.partial(_in_proj_kernel, conv_width=conv_width),
        grid=(b, nt),
        in_specs=[
            pl.BlockSpec((1, tm, d), lambda bi, i: (bi, i, 0)),
            pl.BlockSpec((1, 1, 1, tm), lambda bi, i: (bi, i, 0, 0)),
            _const_spec((1, d)),
            _const_spec((qkv_rows, d)),
            _const_spec((d, 2 * conv_width)),
            _const_spec((QK_DIM, 1)),
            _const_spec((QK_DIM, 1)),
            _const_spec((HALF_ROT, 1)),
        ],
        out_specs=[
            pl.BlockSpec((1, 1, Q_COLS, tm), lambda bi, i: (bi, i, 0, 0)),
            pl.BlockSpec((1, tm, Q_COLS), lambda bi, i: (bi, i, 0)),
            pl.BlockSpec((1, 1, ATTN_WIDTH, tm), lambda bi, i: (bi, i, 0, 0)),
            pl.BlockSpec((1, tm, conv_width), lambda bi, i: (bi, i, 0)),
        ],
        out_shape=[
            jax.ShapeDtypeStruct((b, nt, Q_COLS, tm), BF16),
            jax.ShapeDtypeStruct((b, s, Q_COLS), BF16),
            jax.ShapeDtypeStruct((b, nt, ATTN_WIDTH, tm), BF16),
            jax.ShapeDtypeStruct((b, s, conv_width), F32),
        ],
        compiler_params=pltpu.CompilerParams(
            dimension_semantics=("parallel", "parallel"), vmem_limit_bytes=VMEM_LIMIT_BYTES),
        name="in_proj",
    )(h, pos_f, g1, wqkv_t, wglu, qg, kg, invf)


def _attn_kernel(q_t_ref, k_ref, v_t_ref, lamv_ref, subg_ref, o_ref, acc_ref, *, lam_init):
    t = q_t_ref.shape[-1]
    qi = pl.program_id(2)
    q_t = q_t_ref[0, 0]
    zero_half = jnp.zeros((QK_DIM, t), BF16)
    q_maps = (jnp.concatenate([q_t[:QK_DIM], zero_half], axis=0),
              jnp.concatenate([zero_half, q_t[QK_DIM:]], axis=0))
    acc_ref[...] = jnp.zeros_like(acc_ref)

    def chunk(j, carry, masked):
        kc = k_ref[0, pl.ds(pl.multiple_of(j * t, t), t), :]
        vc = v_t_ref[0, j]
        out = []
        for idx in range(2):
            m, l = carry[2 * idx], carry[2 * idx + 1]
            s = jnp.dot(kc, q_maps[idx], preferred_element_type=F32)
            if masked:
                key_pos = lax.broadcasted_iota(jnp.int32, (t, t), 0)
                q_pos = lax.broadcasted_iota(jnp.int32, (t, t), 1)
                s = jnp.where(key_pos <= q_pos, s, MASK_VALUE)
            m_new = jnp.maximum(m, jnp.max(s, axis=0, keepdims=True))
            alpha = jnp.exp(m - m_new)
            p = jnp.exp(s - m_new)
            l_new = alpha * l + jnp.sum(p, axis=0, keepdims=True)
            pv = jnp.dot(vc, p.astype(BF16), preferred_element_type=F32)
            acc_ref[idx] = alpha * acc_ref[idx] + pv
            out += [m_new, l_new]
        return tuple(out)

    m0 = jnp.full((1, t), MASK_VALUE, F32)
    l0 = jnp.zeros((1, t), F32)
    carry = lax.fori_loop(0, qi, functools.partial(chunk, masked=False), (m0, l0, m0, l0))
    _, l1, _, l2 = chunk(qi, carry, masked=True)

    lamv = lamv_ref[...]
    lam = (jnp.exp(jnp.sum(lamv[0:1] * lamv[1:2], axis=-1, keepdims=True))
           - jnp.exp(jnp.sum(lamv[2:3] * lamv[3:4], axis=-1, keepdims=True)) + lam_init)
    o = acc_ref[0] * (1.0 / l1) - lam * (acc_ref[1] * (1.0 / l2))
    ms = jnp.mean(o * o, axis=0, keepdims=True)
    y = o * lax.rsqrt(ms + NORM_EPS) * subg_ref[...] * (1.0 - lam_init)
    o_ref[0] = y.T.astype(BF16)


def _attn(q_t, k, v_t, lamv, subg, lam_init):
    b, nt, _, t = q_t.shape
    s = k.shape[1]
    return pl.pallas_call(
        functools.partial(_attn_kernel, lam_init=lam_init),
        grid=(b, N_HEADS, nt),
        in_specs=[
            pl.BlockSpec((1, 1, 2 * QK_DIM, t), lambda bi, hi, qi: (bi, qi, hi, 0)),
            pl.BlockSpec((1, s, 2 * QK_DIM), lambda bi, hi, qi: (bi, 0, hi)),
            pl.BlockSpec((1, nt, V_DIM, t), lambda bi, hi, qi: (bi, 0, hi, 0)),
            _const_spec((4, QK_DIM)),
            _const_spec((V_DIM, 1)),
        ],
        out_specs=pl.BlockSpec((1, t, V_DIM), lambda bi, hi, qi: (bi, qi, hi)),
        out_shape=jax.ShapeDtypeStruct((b, s, ATTN_WIDTH), BF16),
        scratch_shapes=[pltpu.VMEM((2, V_DIM, t), F32)],
        compiler_params=pltpu.CompilerParams(
            dimension_semantics=("parallel", "parallel", "arbitrary"),
            vmem_limit_bytes=VMEM_LIMIT_BYTES),
        name="attn",
    )(q_t, k, v_t, lamv, subg)


def _mix_kernel(h_ref, attn_ref, c_ref, halo_ref, cw_ref, cb_ref, lng_ref, lnb_ref,
                wo_attn_ref, wo_conv_ref, o_ref, buf_ref):
    tm = c_ref.shape[1]
    taps = cw_ref.shape[0]
    first = pl.program_id(1) == 0
    halo = halo_ref[0]
    buf_ref[:CONV_HALO] = jnp.where(first, jnp.zeros_like(halo), halo)
    buf_ref[CONV_HALO:] = c_ref[0]
    base = CONV_HALO - (taps - 1)
    cw = cw_ref[...]
    y = cb_ref[...] + cw[0:1] * buf_ref[pl.ds(base, tm), :]
    for j in range(1, taps):
        y = y + cw[j:j + 1] * buf_ref[pl.ds(base + j, tm), :]
    mu = jnp.mean(y, axis=-1, keepdims=True)
    yc = y - mu
    z = yc * lax.rsqrt(jnp.mean(yc * yc, axis=-1, keepdims=True) + NORM_EPS)
    z = z * lng_ref[...] + lnb_ref[...]
    z = (z * jax.nn.sigmoid(z)).astype(BF16)
    upd = (jnp.dot(attn_ref[0], wo_attn_ref[...], preferred_element_type=F32)
           + jnp.dot(z, wo_conv_ref[...], preferred_element_type=F32))
    o_ref[0] = h_ref[0] + upd


def _mix(h, attn, c, cw, cb, lng, lnb, wo_attn, wo_conv):
    b, s, d = h.shape
    tm = ROW_TILE
    cwid = c.shape[-1]
    halo_blocks = tm // CONV_HALO
    return pl.pallas_call(
        _mix_kernel,
        grid=(b, s // tm),
        in_specs=[
            pl.BlockSpec((1, tm, d), lambda bi, i: (bi, i, 0)),
            pl.BlockSpec((1, tm, ATTN_WIDTH), lambda bi, i: (bi, i, 0)),
            pl.BlockSpec((1, tm, cwid), lambda bi, i: (bi, i, 0)),
            pl.BlockSpec((1, CONV_HALO, cwid),
                         lambda bi, i: (bi, jnp.maximum(i * halo_blocks - 1, 0), 0)),
            _const_spec(cw.shape),
            _const_spec((1, cwid)),
            _const_spec((1, cwid)),
            _const_spec((1, cwid)),
            _const_spec(wo_attn.shape),
            _const_spec(wo_conv.shape),
        ],
        out_specs=pl.BlockSpec((1, tm, d), lambda bi, i: (bi, i, 0)),
        out_shape=jax.ShapeDtypeStruct((b, s, d), F32),
        scratch_shapes=[pltpu.VMEM((CONV_HALO + tm, cwid), F32)],
        compiler_params=pltpu.CompilerParams(
            dimension_semantics=("parallel", "parallel"), vmem_limit_bytes=VMEM_LIMIT_BYTES),
        name="mix",
    )(h, attn, c, c, cw, cb, lng, lnb, wo_attn, wo_conv)


def _ffn_kernel(h_ref, halo_ref, p_ref, g2_ref, wa_ref, wb_ref, fwa_ref, fwb_ref, fba_ref,
                fbb_ref, wd_ref, pg_ref, wg_ref, bg_ref, wp_ref, o_ref):
    tm = h_ref.shape[1]
    first = pl.program_id(1) == 0
    h = h_ref[0]
    halo = halo_ref[0]
    halo = jnp.where(first, jnp.zeros_like(halo), halo)
    x = jnp.concatenate([halo, h], axis=0)
    xn = _rms_rows(x, g2_ref[...]).astype(BF16)

    def conv_up(w_ref, cw_ref, cb_ref):
        u = jnp.dot(xn, w_ref[...], preferred_element_type=F32)
        cw = cw_ref[...]
        y = cb_ref[...] + cw[2:3] * u[FFN_HALO:]
        y = y + cw[1:2] * u[FFN_HALO - 1:FFN_HALO - 1 + tm]
        y = y + cw[0:1] * u[FFN_HALO - 2:FFN_HALO - 2 + tm]
        return y

    ya = conv_up(wa_ref, fwa_ref, fba_ref)
    yb = conv_up(wb_ref, fwb_ref, fbb_ref)
    act = (ya * jax.nn.sigmoid(ya) * yb).astype(BF16)
    h = h + jnp.dot(act, wd_ref[...], preferred_element_type=F32)

    hn = _rms_rows(h, pg_ref[...]).astype(BF16)
    gate = jax.nn.sigmoid(jnp.dot(hn, wg_ref[...], preferred_element_type=F32) + bg_ref[...])
    ple = jnp.dot(p_ref[0].astype(BF16), wp_ref[...], preferred_element_type=F32)
    o_ref[0] = h + gate * ple


def _ffn(h, p, g2, wa, wb, fwa, fwb, fba, fbb, wd, pg, wg, bg, wp):
    b, s, d = h.shape
    tm = ROW_TILE
    halo_blocks = tm // FFN_HALO
    consts = (g2, wa, wb, fwa, fwb, fba, fbb, wd, pg, wg, bg, wp)
    return pl.pallas_call(
        _ffn_kernel,
        grid=(b, s // tm),
        in_specs=[
            pl.BlockSpec((1, tm, d), lambda bi, i: (bi, i, 0)),
            pl.BlockSpec((1, FFN_HALO, d),
                         lambda bi, i: (bi, jnp.maximum(i * halo_blocks - 1, 0), 0)),
            pl.BlockSpec((1, tm, p.shape[-1]), lambda bi, i: (bi, i, 0)),
        ] + [_const_spec(a.shape) for a in consts],
        out_specs=pl.BlockSpec((1, tm, d), lambda bi, i: (bi, i, 0)),
        out_shape=jax.ShapeDtypeStruct((b, s, d), F32),
        compiler_params=pltpu.CompilerParams(
            dimension_semantics=("parallel", "parallel"), vmem_limit_bytes=VMEM_LIMIT_BYTES),
        name="ffn",
    )(h, h, p, *consts)


def _pad_cols(a, width):
    return jnp.pad(a, ((0, 0), (0, width - a.shape[1])))


def kernel(x, p, positions, norm1_g, w_in, q_norm_g, k_norm_g, lam_q1, lam_k1, lam_q2, lam_k2,
           subln_g, conv_w, conv_b, conv_ln_g, conv_ln_b, w_out, norm2_g, w_up, ffn_conv_w,
           ffn_conv_b, w_down, ple_norm_g, w_ple_gate, b_ple_gate, w_ple):
    b, s, d = x.shape
    depth = w_in.shape[0]
    assert s % ROW_TILE == 0 and d % LANES == 0
    d_ff = w_down.shape[1]
    d_ff_pad = -(-d_ff // LANES) * LANES
    nt = s // ROW_TILE

    pos_f = positions.astype(F32).reshape(b, nt, 1, ROW_TILE)
    invf = (ROPE_THETA ** (-jnp.arange(0, ROT_DIM, 2, dtype=F32) / ROT_DIM)).reshape(HALF_ROT, 1)
    row = lambda v: v.reshape(1, -1).astype(F32)
    col = lambda v: v.reshape(-1, 1).astype(F32)

    h = x
    for i in range(depth):
        lam_init = 0.8 - 0.6 * math.exp(-0.3 * i)
        qkv_cols = 2 * Q_COLS + ATTN_WIDTH
        wqkv_t = w_in[i][:, :qkv_cols].T.astype(BF16)
        wglu = w_in[i][:, qkv_cols:].astype(BF16)
        q_t, k, v_t, c = _in_proj(h, pos_f, row(norm1_g[i]), wqkv_t, wglu,
                                  col(q_norm_g[i]), col(k_norm_g[i]), invf)

        lamv = jnp.stack([lam_q1[i], lam_k1[i], lam_q2[i], lam_k2[i]]).astype(F32)
        attn = _attn(q_t, k, v_t, lamv, col(subln_g[i]), lam_init)

        wo = w_out[i].astype(BF16)
        h = _mix(h, attn, c, conv_w[i].astype(F32), row(conv_b[i]), row(conv_ln_g[i]),
                 row(conv_ln_b[i]), wo[:ATTN_WIDTH], wo[ATTN_WIDTH:])

        wa = _pad_cols(w_up[i][:, :d_ff], d_ff_pad).astype(BF16)
        wb = _pad_cols(w_up[i][:, d_ff:], d_ff_pad).astype(BF16)
        fwa = _pad_cols(ffn_conv_w[i][:, :d_ff], d_ff_pad).astype(F32)
        fwb = _pad_cols(ffn_conv_w[i][:, d_ff:], d_ff_pad).astype(F32)
        fba = _pad_cols(row(ffn_conv_b[i][:d_ff]), d_ff_pad)
        fbb = _pad_cols(row(ffn_conv_b[i][d_ff:]), d_ff_pad)
        wd = jnp.pad(w_down[i], ((0, d_ff_pad - d_ff), (0, 0))).astype(BF16)
        h = _ffn(h, p[i], row(norm2_g[i]), wa, wb, fwa, fwb, fba, fbb, wd,
                 row(ple_norm_g[i]), w_ple_gate[i].astype(BF16), row(b_ple_gate[i]),
                 w_ple[i].astype(BF16))
    return h
```

```python
import functools
import math

import jax
import jax.numpy as jnp
from jax import lax
from jax.experimental import pallas as pl
from jax.experimental.pallas import tpu as pltpu

N_HEADS = 4
QK_DIM = 64
V_DIM = 2 * QK_DIM
ATTN_WIDTH = N_HEADS * V_DIM
Q_COLS = N_HEADS * 2 * QK_DIM
ROT_DIM = QK_DIM // 4
HALF_ROT = ROT_DIM // 2
ROPE_THETA = 500000.0
NORM_EPS = 1e-6
MASK_VALUE = -1e30
LOG2_E = math.log2(math.e)
V_AUG = V_DIM + 16

LANES = 128
SUBLANES = 8
VMEM_LIMIT_BYTES = 56 * 1024 * 1024

ROW_TILE = 512
Q_CHUNKS = 2
CONV_HALO = 32
FFN_HALO = 8

F32 = jnp.float32
BF16 = jnp.bfloat16


def _const_spec(shape):
    return pl.BlockSpec(shape, lambda *_: (0,) * len(shape), pipeline_mode=pl.Buffered(1))


def _rms_rows(x, g):
    ms = jnp.mean(x * x, axis=-1, keepdims=True)
    return x * lax.rsqrt(ms + NORM_EPS) * g


def _in_proj_kernel(h_ref, pos_ref, g1_ref, wqkv_t_ref, wglu_ref, qg_ref, kg_ref, invf_ref,
                    q_t_ref, k_ref, v_t_ref, c_ref, *, conv_width):
    hn = _rms_rows(h_ref[0], g1_ref[...]).astype(BF16)
    u_t = lax.dot_general(wqkv_t_ref[...], hn, (((1,), (1,)), ((), ())),
                          preferred_element_type=F32)
    glu = jnp.dot(hn, wglu_ref[...], preferred_element_type=F32)
    c_ref[0] = glu[:, :conv_width] * jax.nn.sigmoid(glu[:, conv_width:])

    ang = invf_ref[...] * pos_ref[0, 0]
    cos, sin = jnp.cos(ang), jnp.sin(ang)

    def norm_rope(x_t, g):
        segs = []
        for s in range(Q_COLS // QK_DIM):
            seg = x_t[s * QK_DIM:(s + 1) * QK_DIM]
            ms = jnp.mean(seg * seg, axis=0, keepdims=True)
            y = seg * lax.rsqrt(ms + NORM_EPS) * g
            x1, x2 = y[:HALF_ROT], y[HALF_ROT:ROT_DIM]
            segs += [x1 * cos - x2 * sin, x2 * cos + x1 * sin, y[ROT_DIM:]]
        return jnp.concatenate(segs, axis=0)

    q_t = norm_rope(u_t[:Q_COLS], qg_ref[...]) * (QK_DIM ** -0.5 * LOG2_E)
    k_t = norm_rope(u_t[Q_COLS:2 * Q_COLS], kg_ref[...])
    q_t_ref[0, 0] = q_t.astype(BF16)
    k_ref[0] = k_t.T.astype(BF16)
    ones = jnp.ones((V_AUG - V_DIM, u_t.shape[1]), BF16)
    for hd in range(N_HEADS):
        v_rows = u_t[2 * Q_COLS + hd * V_DIM:2 * Q_COLS + (hd + 1) * V_DIM]
        v_t_ref[0, 0, hd, :V_DIM] = v_rows.astype(BF16)
        v_t_ref[0, 0, hd, V_DIM:] = ones


def _in_proj(h, pos_f, g1, wqkv_t, wglu, qg, kg, invf):
    b, s, d = h.shape
    tm = ROW_TILE
    nt = s // tm
    conv_width = wglu.shape[1] // 2
    qkv_rows = wqkv_t.shape[0]
    return pl.pallas_call(
        functools.partial(_in_proj_kernel, conv_width=conv_width),
        grid=(b, nt),
        in_specs=[
            pl.BlockSpec((1, tm, d), lambda bi, i: (bi, i, 0)),
            pl.BlockSpec((1, 1, 1, tm), lambda bi, i: (bi, i, 0, 0)),
            _const_spec((1, d)),
            _const_spec((qkv_rows, d)),
            _const_spec((d, 2 * conv_width)),
            _const_spec((QK_DIM, 1)),
            _const_spec((QK_DIM, 1)),
            _const_spec((HALF_ROT, 1)),
        ],
        out_specs=[
            pl.BlockSpec((1, 1, Q_COLS, tm), lambda bi, i: (bi, i, 0, 0)),
            pl.BlockSpec((1, tm, Q_COLS), lambda bi, i: (bi, i, 0)),
            pl.BlockSpec((1, 1, N_HEADS, V_AUG, tm), lambda bi, i: (bi, i, 0, 0, 0)),
            pl.BlockSpec((1, tm, conv_width), lambda bi, i: (bi, i, 0)),
        ],
        out_shape=[
            jax.ShapeDtypeStruct((b, nt, Q_COLS, tm), BF16),
            jax.ShapeDtypeStruct((b, s, Q_COLS), BF16),
            jax.ShapeDtypeStruct((b, nt, N_HEADS, V_AUG, tm), BF16),
            jax.ShapeDtypeStruct((b, s, conv_width), F32),
        ],
        compiler_params=pltpu.CompilerParams(
            dimension_semantics=("parallel", "parallel"), vmem_limit_bytes=VMEM_LIMIT_BYTES),
        name="in_proj",
    )(h, pos_f, g1, wqkv_t, wglu, qg, kg, invf)


def _attn_kernel(q_t_ref, k_ref, v_t_ref, lamv_ref, subg_ref, o_ref, acc_ref, sa_ref, sb_ref,
                 *, lam_init):
    t = k_ref.shape[1] // v_t_ref.shape[1]
    tq = Q_CHUNKS * t
    qi = pl.program_id(2)
    q_t = jnp.concatenate([q_t_ref[0, c] for c in range(Q_CHUNKS)], axis=1)
    zero_half = jnp.zeros((QK_DIM, tq), BF16)
    q_maps = (jnp.concatenate([q_t[:QK_DIM], zero_half], axis=0),
              jnp.concatenate([zero_half, q_t[QK_DIM:]], axis=0))
    acc_ref[...] = jnp.zeros_like(acc_ref)

    def scores(j, s_ref):
        kc = k_ref[0, pl.ds(pl.multiple_of(j * t, t), t), :]
        for idx in range(2):
            s_ref[idx] = jnp.dot(kc, q_maps[idx], preferred_element_type=F32)

    def update(j, s_ref, ms, masked):
        vc = v_t_ref[0, j, 0]
        out = []
        for idx in range(2):
            s = s_ref[idx]
            if masked:
                key_pos = j * t + lax.broadcasted_iota(jnp.int32, (t, tq), 0)
                q_pos = qi * tq + lax.broadcasted_iota(jnp.int32, (t, tq), 1)
                s = jnp.where(key_pos <= q_pos, s, MASK_VALUE)
            m_new = jnp.maximum(ms[idx], jnp.max(s, axis=0, keepdims=True))
            alpha = jnp.exp2(ms[idx] - m_new)
            p = jnp.exp2((s - m_new).astype(BF16))
            acc_ref[idx] = alpha * acc_ref[idx] + jnp.dot(vc, p, preferred_element_type=F32)
            out.append(m_new)
        return tuple(out)

    def body(jj, ms):
        c0 = Q_CHUNKS * jj
        scores(c0 + 1, sb_ref)
        ms = update(c0, sa_ref, ms, masked=False)
        scores(c0 + 2, sa_ref)
        return update(c0 + 1, sb_ref, ms, masked=False)

    m0 = jnp.full((1, tq), MASK_VALUE, F32)
    scores(0, sa_ref)
    ms = lax.fori_loop(0, qi, body, (m0, m0))
    d0 = Q_CHUNKS * qi
    scores(d0 + 1, sb_ref)
    ms = update(d0, sa_ref, ms, masked=True)
    update(d0 + 1, sb_ref, ms, masked=True)

    lamv = lamv_ref[...]
    lam = (jnp.exp(jnp.sum(lamv[0:1] * lamv[1:2], axis=-1, keepdims=True))
           - jnp.exp(jnp.sum(lamv[2:3] * lamv[3:4], axis=-1, keepdims=True)) + lam_init)
    o1 = acc_ref[0, :V_DIM] * (1.0 / acc_ref[0, V_DIM:V_DIM + 1])
    o2 = acc_ref[1, :V_DIM] * (1.0 / acc_ref[1, V_DIM:V_DIM + 1])
    o = o1 - lam * o2
    ms_o = jnp.mean(o * o, axis=0, keepdims=True)
    y = o * lax.rsqrt(ms_o + NORM_EPS) * subg_ref[...] * (1.0 - lam_init)
    o_ref[0] = y.T.astype(BF16)


def _attn(q_t, k, v_t, lamv, subg, lam_init):
    b, nt, _, t = q_t.shape
    s = k.shape[1]
    tq = Q_CHUNKS * t
    return pl.pallas_call(
        functools.partial(_attn_kernel, lam_init=lam_init),
        grid=(b, N_HEADS, s // tq),
        in_specs=[
            pl.BlockSpec((1, Q_CHUNKS, 2 * QK_DIM, t), lambda bi, hi, qi: (bi, qi, hi, 0)),
            pl.BlockSpec((1, s, 2 * QK_DIM), lambda bi, hi, qi: (bi, 0, hi)),
            pl.BlockSpec((1, nt, 1, V_AUG, t), lambda bi, hi, qi: (bi, 0, hi, 0, 0)),
            _const_spec((4, QK_DIM)),
            _const_spec((V_DIM, 1)),
        ],
        out_specs=pl.BlockSpec((1, tq, V_DIM), lambda bi, hi, qi: (bi, qi, hi)),
        out_shape=jax.ShapeDtypeStruct((b, s, ATTN_WIDTH), BF16),
        scratch_shapes=[pltpu.VMEM((2, V_AUG, tq), F32),
                        pltpu.VMEM((2, t, tq), F32),
                        pltpu.VMEM((2, t, tq), F32)],
        compiler_params=pltpu.CompilerParams(
            dimension_semantics=("parallel", "parallel", "arbitrary"),
            vmem_limit_bytes=VMEM_LIMIT_BYTES),
        name="attn",
    )(q_t, k, v_t, lamv, subg)


def _mix_kernel(h_ref, attn_ref, c_ref, halo_ref, cw_ref, cb_ref, lng_ref, lnb_ref,
                wo_attn_ref, wo_conv_ref, o_ref, buf_ref):
    tm = c_ref.shape[1]
    taps = cw_ref.shape[0]
    first = pl.program_id(1) == 0
    halo = halo_ref[0]
    buf_ref[:CONV_HALO] = jnp.where(first, jnp.zeros_like(halo), halo)
    buf_ref[CONV_HALO:] = c_ref[0]
    base = CONV_HALO - (taps - 1)
    cw = cw_ref[...]
    y = cb_ref[...] + cw[0:1] * buf_ref[pl.ds(base, tm), :]
    for j in range(1, taps):
        y = y + cw[j:j + 1] * buf_ref[pl.ds(base + j, tm), :]
    mu = jnp.mean(y, axis=-1, keepdims=True)
    yc = y - mu
    z = yc * lax.rsqrt(jnp.mean(yc * yc, axis=-1, keepdims=True) + NORM_EPS)
    z = z * lng_ref[...] + lnb_ref[...]
    z = (z * jax.nn.sigmoid(z)).astype(BF16)
    upd = (jnp.dot(attn_ref[0], wo_attn_ref[...], preferred_element_type=F32)
           + jnp.dot(z, wo_conv_ref[...], preferred_element_type=F32))
    o_ref[0] = h_ref[0] + upd


def _mix(h, attn, c, cw, cb, lng, lnb, wo_attn, wo_conv):
    b, s, d = h.shape
    tm = ROW_TILE
    cwid = c.shape[-1]
    halo_blocks = tm // CONV_HALO
    return pl.pallas_call(
        _mix_kernel,
        grid=(b, s // tm),
        in_specs=[
            pl.BlockSpec((1, tm, d), lambda bi, i: (bi, i, 0)),
            pl.BlockSpec((1, tm, ATTN_WIDTH), lambda bi, i: (bi, i, 0)),
            pl.BlockSpec((1, tm, cwid), lambda bi, i: (bi, i, 0)),
            pl.BlockSpec((1, CONV_HALO, cwid),
                         lambda bi, i: (bi, jnp.maximum(i * halo_blocks - 1, 0), 0)),
            _const_spec(cw.shape),
            _const_spec((1, cwid)),
            _const_spec((1, cwid)),
            _const_spec((1, cwid)),
            _const_spec(wo_attn.shape),
            _const_spec(wo_conv.shape),
        ],
        out_specs=pl.BlockSpec((1, tm, d), lambda bi, i: (bi, i, 0)),
        out_shape=jax.ShapeDtypeStruct((b, s, d), F32),
        scratch_shapes=[pltpu.VMEM((CONV_HALO + tm, cwid), F32)],
        compiler_params=pltpu.CompilerParams(
            dimension_semantics=("parallel", "parallel"), vmem_limit_bytes=VMEM_LIMIT_BYTES),
        name="mix",
    )(h, attn, c, c, cw, cb, lng, lnb, wo_attn, wo_conv)


def _ffn_kernel(h_ref, halo_ref, p_ref, g2_ref, wa_ref, wb_ref, fwa_ref, fwb_ref, fba_ref,
                fbb_ref, wd_ref, pg_ref, wg_ref, bg_ref, wp_ref, o_ref):
    tm = h_ref.shape[1]
    first = pl.program_id(1) == 0
    h = h_ref[0]
    halo = halo_ref[0]
    halo = jnp.where(first, jnp.zeros_like(halo), halo)
    x = jnp.concatenate([halo, h], axis=0)
    xn = _rms_rows(x, g2_ref[...]).astype(BF16)

    def conv_up(w_ref, cw_ref, cb_ref):
        u = jnp.dot(xn, w_ref[...], preferred_element_type=F32)
        cw = cw_ref[...]
        y = cb_ref[...] + cw[2:3] * u[FFN_HALO:]
        y = y + cw[1:2] * u[FFN_HALO - 1:FFN_HALO - 1 + tm]
        y = y + cw[0:1] * u[FFN_HALO - 2:FFN_HALO - 2 + tm]
        return y

    ya = conv_up(wa_ref, fwa_ref, fba_ref)
    yb = conv_up(wb_ref, fwb_ref, fbb_ref)
    act = (ya * jax.nn.sigmoid(ya) * yb).astype(BF16)
    h = h + jnp.dot(act, wd_ref[...], preferred_element_type=F32)

    hn = _rms_rows(h, pg_ref[...]).astype(BF16)
    gate = jax.nn.sigmoid(jnp.dot(hn, wg_ref[...], preferred_element_type=F32) + bg_ref[...])
    ple = jnp.dot(p_ref[0].astype(BF16), wp_ref[...], preferred_element_type=F32)
    o_ref[0] = h + gate * ple


def _ffn(h, p, g2, wa, wb, fwa, fwb, fba, fbb, wd, pg, wg, bg, wp):
    b, s, d = h.shape
    tm = ROW_TILE
    halo_blocks = tm // FFN_HALO
    consts = (g2, wa, wb, fwa, fwb, fba, fbb, wd, pg, wg, bg, wp)
    return pl.pallas_call(
        _ffn_kernel,
        grid=(b, s // tm),
        in_specs=[
            pl.BlockSpec((1, tm, d), lambda bi, i: (bi, i, 0)),
            pl.BlockSpec((1, FFN_HALO, d),
                         lambda bi, i: (bi, jnp.maximum(i * halo_blocks - 1, 0), 0)),
            pl.BlockSpec((1, tm, p.shape[-1]), lambda bi, i: (bi, i, 0)),
        ] + [_const_spec(a.shape) for a in consts],
        out_specs=pl.BlockSpec((1, tm, d), lambda bi, i: (bi, i, 0)),
        out_shape=jax.ShapeDtypeStruct((b, s, d), F32),
        compiler_params=pltpu.CompilerParams(
            dimension_semantics=("parallel", "parallel"), vmem_limit_bytes=VMEM_LIMIT_BYTES),
        name="ffn",
    )(h, h, p, *consts)


def _pad_cols(a, width):
    return jnp.pad(a, ((0, 0), (0, width - a.shape[1])))


def kernel(x, p, positions, norm1_g, w_in, q_norm_g, k_norm_g, lam_q1, lam_k1, lam_q2, lam_k2,
           subln_g, conv_w, conv_b, conv_ln_g, conv_ln_b, w_out, norm2_g, w_up, ffn_conv_w,
           ffn_conv_b, w_down, ple_norm_g, w_ple_gate, b_ple_gate, w_ple):
    b, s, d = x.shape
    depth = w_in.shape[0]
    assert s % (Q_CHUNKS * ROW_TILE) == 0 and d % LANES == 0
    d_ff = w_down.shape[1]
    d_ff_pad = -(-d_ff // LANES) * LANES
    nt = s // ROW_TILE

    pos_f = positions.astype(F32).reshape(b, nt, 1, ROW_TILE)
    invf = (ROPE_THETA ** (-jnp.arange(0, ROT_DIM, 2, dtype=F32) / ROT_DIM)).reshape(HALF_ROT, 1)
    row = lambda v: v.reshape(1, -1).astype(F32)
    col = lambda v: v.reshape(-1, 1).astype(F32)

    h = x
    for i in range(depth):
        lam_init = 0.8 - 0.6 * math.exp(-0.3 * i)
        qkv_cols = 2 * Q_COLS + ATTN_WIDTH
        wqkv_t = w_in[i][:, :qkv_cols].T.astype(BF16)
        wglu = w_in[i][:, qkv_cols:].astype(BF16)
        q_t, k, v_t, c = _in_proj(h, pos_f, row(norm1_g[i]), wqkv_t, wglu,
                                  col(q_norm_g[i]), col(k_norm_g[i]), invf)

        lamv = jnp.stack([lam_q1[i], lam_k1[i], lam_q2[i], lam_k2[i]]).astype(F32)
        attn = _attn(q_t, k, v_t, lamv, col(subln_g[i]), lam_init)

        wo = w_out[i].astype(BF16)
        h = _mix(h, attn, c, conv_w[i].astype(F32), row(conv_b[i]), row(conv_ln_g[i]),
                 row(conv_ln_b[i]), wo[:ATTN_WIDTH], wo[ATTN_WIDTH:])

        wa = _pad_cols(w_up[i][:, :d_ff], d_ff_pad).astype(BF16)
        wb = _pad_cols(w_up[i][:, d_ff:], d_ff_pad).astype(BF16)
        fwa = _pad_cols(ffn_conv_w[i][:, :d_ff], d_ff_pad).astype(F32)
        fwb = _pad_cols(ffn_conv_w[i][:, d_ff:], d_ff_pad).astype(F32)
        fba = _pad_cols(row(ffn_conv_b[i][:d_ff]), d_ff_pad)
        fbb = _pad_cols(row(ffn_conv_b[i][d_ff:]), d_ff_pad)
        wd = jnp.pad(w_down[i], ((0, d_ff_pad - d_ff), (0, 0))).astype(BF16)
        h = _ffn(h, p[i], row(norm2_g[i]), wa, wb, fwa, fwb, fba, fbb, wd,
                 row(ple_norm_g[i]), w_ple_gate[i].astype(BF16), row(b_ple_gate[i]),
                 w_ple[i].astype(BF16))
    return h
```

```python
import functools
import math

import jax
import jax.numpy as jnp
from jax import lax
from jax.experimental import pallas as pl
from jax.experimental.pallas import tpu as pltpu

N_HEADS = 4
QK_DIM = 64
V_DIM = 2 * QK_DIM
ATTN_WIDTH = N_HEADS * V_DIM
Q_COLS = N_HEADS * 2 * QK_DIM
ROT_DIM = QK_DIM // 4
HALF_ROT = ROT_DIM // 2
ROPE_THETA = 500000.0
NORM_EPS = 1e-6
MASK_VALUE = -1e30
LOG2_E = math.log2(math.e)
V_AUG = V_DIM + 16

LANES = 128
SUBLANES = 8
VMEM_LIMIT_BYTES = 56 * 1024 * 1024

ROW_TILE = 512
Q_CHUNKS = 2
STRIP = 256
CONV_HALO = 32
FFN_HALO = 8

F32 = jnp.float32
BF16 = jnp.bfloat16


def _const_spec(shape):
    return pl.BlockSpec(shape, lambda *_: (0,) * len(shape), pipeline_mode=pl.Buffered(1))


def _rms_rows(x, g):
    ms = jnp.mean(x * x, axis=-1, keepdims=True)
    return x * lax.rsqrt(ms + NORM_EPS) * g


def _in_proj_kernel(h_ref, pos_ref, g1_ref, wqkv_t_ref, wglu_ref, qg_ref, kg_ref, invf_ref,
                    q_t_ref, k_ref, v_t_ref, c_ref, *, conv_width):
    hn = _rms_rows(h_ref[0], g1_ref[...]).astype(BF16)
    u_t = lax.dot_general(wqkv_t_ref[...], hn, (((1,), (1,)), ((), ())),
                          preferred_element_type=F32)
    glu = jnp.dot(hn, wglu_ref[...], preferred_element_type=F32)
    c_ref[0] = glu[:, :conv_width] * jax.nn.sigmoid(glu[:, conv_width:])

    ang = invf_ref[...] * pos_ref[0, 0]
    cos, sin = jnp.cos(ang), jnp.sin(ang)

    def norm_rope(x_t, g):
        segs = []
        for s in range(Q_COLS // QK_DIM):
            seg = x_t[s * QK_DIM:(s + 1) * QK_DIM]
            ms = jnp.mean(seg * seg, axis=0, keepdims=True)
            y = seg * lax.rsqrt(ms + NORM_EPS) * g
            x1, x2 = y[:HALF_ROT], y[HALF_ROT:ROT_DIM]
            segs += [x1 * cos - x2 * sin, x2 * cos + x1 * sin, y[ROT_DIM:]]
        return jnp.concatenate(segs, axis=0)

    q_t = norm_rope(u_t[:Q_COLS], qg_ref[...]) * (QK_DIM ** -0.5 * LOG2_E)
    k_t = norm_rope(u_t[Q_COLS:2 * Q_COLS], kg_ref[...])
    q_t_ref[0, 0] = q_t.astype(BF16)
    k_ref[0] = k_t.T.astype(BF16)
    ones = jnp.ones((V_AUG - V_DIM, u_t.shape[1]), BF16)
    for hd in range(N_HEADS):
        v_rows = u_t[2 * Q_COLS + hd * V_DIM:2 * Q_COLS + (hd + 1) * V_DIM]
        v_t_ref[0, 0, hd, :V_DIM] = v_rows.astype(BF16)
        v_t_ref[0, 0, hd, V_DIM:] = ones


def _in_proj(h, pos_f, g1, wqkv_t, wglu, qg, kg, invf):
    b, s, d = h.shape
    tm = ROW_TILE
    nt = s // tm
    conv_width = wglu.shape[1] // 2
    qkv_rows = wqkv_t.shape[0]
    return pl.pallas_call(
        functools.partial(_in_proj_kernel, conv_width=conv_width),
        grid=(b, nt),
        in_specs=[
            pl.BlockSpec((1, tm, d), lambda bi, i: (bi, i, 0)),
            pl.BlockSpec((1, 1, 1, tm), lambda bi, i: (bi, i, 0, 0)),
            _const_spec((1, d)),
            _const_spec((qkv_rows, d)),
            _const_spec((d, 2 * conv_width)),
            _const_spec((QK_DIM, 1)),
            _const_spec((QK_DIM, 1)),
            _const_spec((HALF_ROT, 1)),
        ],
        out_specs=[
            pl.BlockSpec((1, 1, Q_COLS, tm), lambda bi, i: (bi, i, 0, 0)),
            pl.BlockSpec((1, tm, Q_COLS), lambda bi, i: (bi, i, 0)),
            pl.BlockSpec((1, 1, N_HEADS, V_AUG, tm), lambda bi, i: (bi, i, 0, 0, 0)),
            pl.BlockSpec((1, tm, conv_width), lambda bi, i: (bi, i, 0)),
        ],
        out_shape=[
            jax.ShapeDtypeStruct((b, nt, Q_COLS, tm), BF16),
            jax.ShapeDtypeStruct((b, s, Q_COLS), BF16),
            jax.ShapeDtypeStruct((b, nt, N_HEADS, V_AUG, tm), BF16),
            jax.ShapeDtypeStruct((b, s, conv_width), F32),
        ],
        compiler_params=pltpu.CompilerParams(
            dimension_semantics=("parallel", "parallel"), vmem_limit_bytes=VMEM_LIMIT_BYTES),
        name="in_proj",
    )(h, pos_f, g1, wqkv_t, wglu, qg, kg, invf)


def _attn_kernel(q_t_ref, k_ref, v_t_ref, lamv_ref, subg_ref, o_ref, acc_ref, sa_ref, sb_ref,
                 pa_ref, pb_ref, *, lam_init):
    t = k_ref.shape[1] // v_t_ref.shape[1]
    tq = Q_CHUNKS * t
    qi = pl.program_id(2)
    q_t = jnp.concatenate([q_t_ref[0, c] for c in range(Q_CHUNKS)], axis=1)
    zero_half = jnp.zeros((QK_DIM, tq), BF16)
    q_maps = (jnp.concatenate([q_t[:QK_DIM], zero_half], axis=0),
              jnp.concatenate([zero_half, q_t[QK_DIM:]], axis=0))

    def keys(j):
        return k_ref[0, pl.ds(pl.multiple_of(j * t, t), t), :]

    units = [(idx, slice(c * STRIP, (c + 1) * STRIP)) for c in range(tq // STRIP)
             for idx in range(2)]

    def score_unit(kc, s_ref, idx, cols):
        s = jnp.dot(kc, q_maps[idx][:, cols], preferred_element_type=F32)
        s_ref[idx, :, cols] = s
        return jnp.max(s, axis=0, keepdims=True)

    def numerator_unit(s_ref, p_ref, idx, cols, m_new):
        for r in range(2):
            rows = slice(r * (t // 2), (r + 1) * (t // 2))
            p_ref[idx, rows, cols] = jnp.exp2((s_ref[idx, rows, cols] - m_new).astype(BF16))

    def accumulate_unit(vc, p_ref, idx, cols, alpha):
        acc_ref[idx, :, cols] = alpha * acc_ref[idx, :, cols] + jnp.dot(
            vc, p_ref[idx, :, cols], preferred_element_type=F32)

    def stage(j_acc, p_acc_ref, alphas, s_cur_ref, p_cur_ref, ms, cmax, j_next, s_next_ref):
        vc = v_t_ref[0, j_acc, 0]
        kc = keys(j_next)
        new_ms, new_alphas, new_cmax = [], [], []
        for u, (idx, cols) in enumerate(units):
            new_cmax.append(score_unit(kc, s_next_ref, idx, cols))
            m_new = jnp.maximum(ms[u], cmax[u])
            numerator_unit(s_cur_ref, p_cur_ref, idx, cols, m_new)
            new_ms.append(m_new)
            new_alphas.append(jnp.exp2(ms[u] - m_new))
            accumulate_unit(vc, p_acc_ref, idx, cols, alphas[u])
        return tuple(new_ms), tuple(new_alphas), tuple(new_cmax)

    d0 = Q_CHUNKS * qi
    tri = (lax.broadcasted_iota(jnp.int32, (t, t), 0)
           <= lax.broadcasted_iota(jnp.int32, (t, t), 1))
    acc_ref[...] = jnp.zeros_like(acc_ref)
    kc_first = keys(0)
    kc = keys(d0 + 1)
    ms, alphas, cmax = [], [], []
    m_lo = jnp.full((1, STRIP), MASK_VALUE, F32)
    one = jnp.ones((1, STRIP), F32)
    for idx, cols in units:
        cmax.append(score_unit(kc_first, sa_ref, idx, cols))
        if cols.stop <= t:
            pb_ref[idx, :, cols] = jnp.zeros((t, STRIP), BF16)
            ms.append(m_lo)
        else:
            s = jnp.dot(kc, q_maps[idx][:, cols], preferred_element_type=F32)
            s = jnp.where(tri[:, cols.start - t:cols.stop - t], s, MASK_VALUE)
            m = jnp.max(s, axis=0, keepdims=True)
            pb_ref[idx, :, cols] = jnp.exp2((s - m).astype(BF16))
            ms.append(m)
        alphas.append(one)
    carry = (tuple(ms), tuple(alphas), tuple(cmax))

    def body(jj, carry):
        ms, alphas, cmax = carry
        c0 = Q_CHUNKS * jj
        ms, alphas, cmax = stage(jnp.where(jj == 0, d0 + 1, c0 - 1), pb_ref, alphas,
                                 sa_ref, pa_ref, ms, cmax, c0 + 1, sb_ref)
        return stage(c0, pa_ref, alphas, sb_ref, pb_ref, ms, cmax, c0 + 2, sa_ref)

    ms, alphas, _ = lax.fori_loop(0, qi, body, carry)

    vc_pending = v_t_ref[0, jnp.where(qi == 0, d0 + 1, d0 - 1), 0]
    vc = v_t_ref[0, d0, 0]
    for u, (idx, cols) in enumerate(units):
        accumulate_unit(vc_pending, pb_ref, idx, cols, alphas[u])
        s = sa_ref[idx, :, cols]
        if cols.stop <= t:
            s = jnp.where(tri[:, cols], s, MASK_VALUE)
        m_new = jnp.maximum(ms[u], jnp.max(s, axis=0, keepdims=True))
        pa_ref[idx, :, cols] = jnp.exp2((s - m_new).astype(BF16))
        accumulate_unit(vc, pa_ref, idx, cols, jnp.exp2(ms[u] - m_new))

    lamv = lamv_ref[...]
    lam = (jnp.exp(jnp.sum(lamv[0:1] * lamv[1:2], axis=-1, keepdims=True))
           - jnp.exp(jnp.sum(lamv[2:3] * lamv[3:4], axis=-1, keepdims=True)) + lam_init)
    o1 = acc_ref[0, :V_DIM] * (1.0 / acc_ref[0, V_DIM:V_DIM + 1])
    o2 = acc_ref[1, :V_DIM] * (1.0 / acc_ref[1, V_DIM:V_DIM + 1])
    o = o1 - lam * o2
    ms_o = jnp.mean(o * o, axis=0, keepdims=True)
    y = o * lax.rsqrt(ms_o + NORM_EPS) * subg_ref[...] * (1.0 - lam_init)
    o_ref[0] = y.T.astype(BF16)


def _attn(q_t, k, v_t, lamv, subg, lam_init):
    b, nt, _, t = q_t.shape
    s = k.shape[1]
    tq = Q_CHUNKS * t
    return pl.pallas_call(
        functools.partial(_attn_kernel, lam_init=lam_init),
        grid=(b, N_HEADS, s // tq),
        in_specs=[
            pl.BlockSpec((1, Q_CHUNKS, 2 * QK_DIM, t), lambda bi, hi, qi: (bi, qi, hi, 0)),
            pl.BlockSpec((1, s, 2 * QK_DIM), lambda bi, hi, qi: (bi, 0, hi)),
            pl.BlockSpec((1, nt, 1, V_AUG, t), lambda bi, hi, qi: (bi, 0, hi, 0, 0)),
            _const_spec((4, QK_DIM)),
            _const_spec((V_DIM, 1)),
        ],
        out_specs=pl.BlockSpec((1, tq, V_DIM), lambda bi, hi, qi: (bi, qi, hi)),
        out_shape=jax.ShapeDtypeStruct((b, s, ATTN_WIDTH), BF16),
        scratch_shapes=[pltpu.VMEM((2, V_AUG, tq), F32),
                        pltpu.VMEM((2, t, tq), F32),
                        pltpu.VMEM((2, t, tq), F32),
                        pltpu.VMEM((2, t, tq), BF16),
                        pltpu.VMEM((2, t, tq), BF16)],
        compiler_params=pltpu.CompilerParams(
            dimension_semantics=("parallel", "parallel", "arbitrary"),
            vmem_limit_bytes=VMEM_LIMIT_BYTES),
        name="attn",
    )(q_t, k, v_t, lamv, subg)


def _mix_kernel(h_ref, attn_ref, c_ref, halo_ref, cw_ref, cb_ref, lng_ref, lnb_ref,
                wo_attn_ref, wo_conv_ref, o_ref, buf_ref, shift_ref):
    tm = c_ref.shape[1]
    taps = cw_ref.shape[0]
    first = pl.program_id(1) == 0
    halo = halo_ref[0]
    buf_ref[:CONV_HALO] = jnp.where(first, jnp.zeros_like(halo), halo)
    buf_ref[CONV_HALO:] = c_ref[0]
    base = CONV_HALO - (taps - 1)
    span = shift_ref.shape[1]
    for r in range(1, SUBLANES):
        shift_ref[r] = buf_ref[pl.ds(r, span), :]
    cw = cw_ref[...]
    y = cb_ref[...]
    for j in range(taps):
        r, a = (base + j) % SUBLANES, (base + j) // SUBLANES
        src = buf_ref if r == 0 else shift_ref.at[r]
        y = y + cw[j:j + 1] * src[pl.ds(a * SUBLANES, tm), :]
    mu = jnp.mean(y, axis=-1, keepdims=True)
    yc = y - mu
    z = yc * lax.rsqrt(jnp.mean(yc * yc, axis=-1, keepdims=True) + NORM_EPS)
    z = z * lng_ref[...] + lnb_ref[...]
    z = (z * jax.nn.sigmoid(z)).astype(BF16)
    upd = (jnp.dot(attn_ref[0], wo_attn_ref[...], preferred_element_type=F32)
           + jnp.dot(z, wo_conv_ref[...], preferred_element_type=F32))
    o_ref[0] = h_ref[0] + upd


def _mix(h, attn, c, cw, cb, lng, lnb, wo_attn, wo_conv):
    b, s, d = h.shape
    tm = ROW_TILE
    cwid = c.shape[-1]
    halo_blocks = tm // CONV_HALO
    return pl.pallas_call(
        _mix_kernel,
        grid=(b, s // tm),
        in_specs=[
            pl.BlockSpec((1, tm, d), lambda bi, i: (bi, i, 0)),
            pl.BlockSpec((1, tm, ATTN_WIDTH), lambda bi, i: (bi, i, 0)),
            pl.BlockSpec((1, tm, cwid), lambda bi, i: (bi, i, 0)),
            pl.BlockSpec((1, CONV_HALO, cwid),
                         lambda bi, i: (bi, jnp.maximum(i * halo_blocks - 1, 0), 0)),
            _const_spec(cw.shape),
            _const_spec((1, cwid)),
            _const_spec((1, cwid)),
            _const_spec((1, cwid)),
            _const_spec(wo_attn.shape),
            _const_spec(wo_conv.shape),
        ],
        out_specs=pl.BlockSpec((1, tm, d), lambda bi, i: (bi, i, 0)),
        out_shape=jax.ShapeDtypeStruct((b, s, d), F32),
        scratch_shapes=[pltpu.VMEM((CONV_HALO + tm, cwid), F32),
                        pltpu.VMEM((SUBLANES, CONV_HALO + tm - SUBLANES, cwid), F32)],
        compiler_params=pltpu.CompilerParams(
            dimension_semantics=("parallel", "parallel"), vmem_limit_bytes=VMEM_LIMIT_BYTES),
        name="mix",
    )(h, attn, c, c, cw, cb, lng, lnb, wo_attn, wo_conv)


def _ffn_kernel(h_ref, halo_ref, p_ref, g2_ref, wa_ref, wb_ref, fwa_ref, fwb_ref, fba_ref,
                fbb_ref, wd_ref, pg_ref, wg_ref, bg_ref, wp_ref, o_ref):
    tm = h_ref.shape[1]
    first = pl.program_id(1) == 0
    h = h_ref[0]
    halo = halo_ref[0]
    halo = jnp.where(first, jnp.zeros_like(halo), halo)
    x = jnp.concatenate([halo, h], axis=0)
    xn = _rms_rows(x, g2_ref[...]).astype(BF16)

    def conv_up(w_ref, cw_ref, cb_ref):
        u = jnp.dot(xn, w_ref[...], preferred_element_type=F32)
        cw = cw_ref[...]
        y = cb_ref[...] + cw[2:3] * u[FFN_HALO:]
        y = y + cw[1:2] * u[FFN_HALO - 1:FFN_HALO - 1 + tm]
        y = y + cw[0:1] * u[FFN_HALO - 2:FFN_HALO - 2 + tm]
        return y

    ya = conv_up(wa_ref, fwa_ref, fba_ref)
    yb = conv_up(wb_ref, fwb_ref, fbb_ref)
    act = (ya * jax.nn.sigmoid(ya) * yb).astype(BF16)
    h = h + jnp.dot(act, wd_ref[...], preferred_element_type=F32)

    hn = _rms_rows(h, pg_ref[...]).astype(BF16)
    gate = jax.nn.sigmoid(jnp.dot(hn, wg_ref[...], preferred_element_type=F32) + bg_ref[...])
    ple = jnp.dot(p_ref[0].astype(BF16), wp_ref[...], preferred_element_type=F32)
    o_ref[0] = h + gate * ple


def _ffn(h, p, g2, wa, wb, fwa, fwb, fba, fbb, wd, pg, wg, bg, wp):
    b, s, d = h.shape
    tm = ROW_TILE
    halo_blocks = tm // FFN_HALO
    consts = (g2, wa, wb, fwa, fwb, fba, fbb, wd, pg, wg, bg, wp)
    return pl.pallas_call(
        _ffn_kernel,
        grid=(b, s // tm),
        in_specs=[
            pl.BlockSpec((1, tm, d), lambda bi, i: (bi, i, 0)),
            pl.BlockSpec((1, FFN_HALO, d),
                         lambda bi, i: (bi, jnp.maximum(i * halo_blocks - 1, 0), 0)),
            pl.BlockSpec((1, tm, p.shape[-1]), lambda bi, i: (bi, i, 0)),
        ] + [_const_spec(a.shape) for a in consts],
        out_specs=pl.BlockSpec((1, tm, d), lambda bi, i: (bi, i, 0)),
        out_shape=jax.ShapeDtypeStruct((b, s, d), F32),
        compiler_params=pltpu.CompilerParams(
            dimension_semantics=("parallel", "parallel"), vmem_limit_bytes=VMEM_LIMIT_BYTES),
        name="ffn",
    )(h, h, p, *consts)


def _pad_cols(a, width):
    return jnp.pad(a, ((0, 0), (0, width - a.shape[1])))


def kernel(x, p, positions, norm1_g, w_in, q_norm_g, k_norm_g, lam_q1, lam_k1, lam_q2, lam_k2,
           subln_g, conv_w, conv_b, conv_ln_g, conv_ln_b, w_out, norm2_g, w_up, ffn_conv_w,
           ffn_conv_b, w_down, ple_norm_g, w_ple_gate, b_ple_gate, w_ple):
    b, s, d = x.shape
    depth = w_in.shape[0]
    assert s % (Q_CHUNKS * ROW_TILE) == 0 and d % LANES == 0
    d_ff = w_down.shape[1]
    d_ff_pad = -(-d_ff // LANES) * LANES
    nt = s // ROW_TILE

    pos_f = positions.astype(F32).reshape(b, nt, 1, ROW_TILE)
    invf = (ROPE_THETA ** (-jnp.arange(0, ROT_DIM, 2, dtype=F32) / ROT_DIM)).reshape(HALF_ROT, 1)
    row = lambda v: v.reshape(1, -1).astype(F32)
    col = lambda v: v.reshape(-1, 1).astype(F32)

    h = x
    for i in range(depth):
        lam_init = 0.8 - 0.6 * math.exp(-0.3 * i)
        qkv_cols = 2 * Q_COLS + ATTN_WIDTH
        wqkv_t = w_in[i][:, :qkv_cols].T.astype(BF16)
        wglu = w_in[i][:, qkv_cols:].astype(BF16)
        q_t, k, v_t, c = _in_proj(h, pos_f, row(norm1_g[i]), wqkv_t, wglu,
                                  col(q_norm_g[i]), col(k_norm_g[i]), invf)

        lamv = jnp.stack([lam_q1[i], lam_k1[i], lam_q2[i], lam_k2[i]]).astype(F32)
        attn = _attn(q_t, k, v_t, lamv, col(subln_g[i]), lam_init)

        wo = w_out[i].astype(BF16)
        h = _mix(h, attn, c, conv_w[i].astype(F32), row(conv_b[i]), row(conv_ln_g[i]),
                 row(conv_ln_b[i]), wo[:ATTN_WIDTH], wo[ATTN_WIDTH:])

        wa = _pad_cols(w_up[i][:, :d_ff], d_ff_pad).astype(BF16)
        wb = _pad_cols(w_up[i][:, d_ff:], d_ff_pad).astype(BF16)
        fwa = _pad_cols(ffn_conv_w[i][:, :d_ff], d_ff_pad).astype(F32)
        fwb = _pad_cols(ffn_conv_w[i][:, d_ff:], d_ff_pad).astype(F32)
        fba = _pad_cols(row(ffn_conv_b[i][:d_ff]), d_ff_pad)
        fbb = _pad_cols(row(ffn_conv_b[i][d_ff:]), d_ff_pad)
        wd = jnp.pad(w_down[i], ((0, d_ff_pad - d_ff), (0, 0))).astype(BF16)
        h = _ffn(h, p[i], row(norm2_g[i]), wa, wb, fwa, fwb, fba, fbb, wd,
                 row(ple_norm_g[i]), w_ple_gate[i].astype(BF16), row(b_ple_gate[i]),
                 w_ple[i].astype(BF16))
    return h
```

```python
import functools
import math

import jax
import jax.numpy as jnp
from jax import lax
from jax.experimental import pallas as pl
from jax.experimental.pallas import tpu as pltpu

N_HEADS = 4
QK_DIM = 64
V_DIM = 2 * QK_DIM
ATTN_WIDTH = N_HEADS * V_DIM
Q_COLS = N_HEADS * 2 * QK_DIM
ROT_DIM = QK_DIM // 4
HALF_ROT = ROT_DIM // 2
ROPE_THETA = 500000.0
NORM_EPS = 1e-6
MASK_VALUE = -1e30
LOG2_E = math.log2(math.e)
V_AUG = V_DIM + 16

LANES = 128
SUBLANES = 8
VMEM_LIMIT_BYTES = 56 * 1024 * 1024

ROW_TILE = 512
Q_CHUNKS = 2
STRIP = 256
SCORE_BOUND = 64.0
CONV_HALO = 32
FFN_HALO = 8

F32 = jnp.float32
BF16 = jnp.bfloat16


def _const_spec(shape):
    return pl.BlockSpec(shape, lambda *_: (0,) * len(shape), pipeline_mode=pl.Buffered(1))


def _rms_rows(x, g):
    ms = jnp.mean(x * x, axis=-1, keepdims=True)
    return x * lax.rsqrt(ms + NORM_EPS) * g


def _in_proj_kernel(h_ref, pos_ref, g1_ref, wqkv_t_ref, wglu_ref, qg_ref, kg_ref, invf_ref,
                    q_t_ref, k_ref, v_t_ref, c_ref, *, conv_width):
    hn = _rms_rows(h_ref[0], g1_ref[...]).astype(BF16)
    u_t = lax.dot_general(wqkv_t_ref[...], hn, (((1,), (1,)), ((), ())),
                          preferred_element_type=F32)
    glu = jnp.dot(hn, wglu_ref[...], preferred_element_type=F32)
    c_ref[0] = glu[:, :conv_width] * jax.nn.sigmoid(glu[:, conv_width:])

    ang = invf_ref[...] * pos_ref[0, 0]
    cos, sin = jnp.cos(ang), jnp.sin(ang)

    def norm_rope(x_t, g):
        segs = []
        for s in range(Q_COLS // QK_DIM):
            seg = x_t[s * QK_DIM:(s + 1) * QK_DIM]
            ms = jnp.mean(seg * seg, axis=0, keepdims=True)
            y = seg * lax.rsqrt(ms + NORM_EPS) * g
            x1, x2 = y[:HALF_ROT], y[HALF_ROT:ROT_DIM]
            segs += [x1 * cos - x2 * sin, x2 * cos + x1 * sin, y[ROT_DIM:]]
        return jnp.concatenate(segs, axis=0)

    q_t = norm_rope(u_t[:Q_COLS], qg_ref[...]) * (QK_DIM ** -0.5 * LOG2_E)
    k_t = norm_rope(u_t[Q_COLS:2 * Q_COLS], kg_ref[...])
    q_t_ref[0, 0] = q_t.astype(BF16)
    k_ref[0] = k_t.T.astype(BF16)
    ones = jnp.ones((V_AUG - V_DIM, u_t.shape[1]), BF16)
    for hd in range(N_HEADS):
        v_rows = u_t[2 * Q_COLS + hd * V_DIM:2 * Q_COLS + (hd + 1) * V_DIM]
        v_t_ref[0, 0, hd, :V_DIM] = v_rows.astype(BF16)
        v_t_ref[0, 0, hd, V_DIM:] = ones


def _in_proj(h, pos_f, g1, wqkv_t, wglu, qg, kg, invf):
    b, s, d = h.shape
    tm = ROW_TILE
    nt = s // tm
    conv_width = wglu.shape[1] // 2
    qkv_rows = wqkv_t.shape[0]
    return pl.pallas_call(
        functools.partial(_in_proj_kernel, conv_width=conv_width),
        grid=(b, nt),
        in_specs=[
            pl.BlockSpec((1, tm, d), lambda bi, i: (bi, i, 0)),
            pl.BlockSpec((1, 1, 1, tm), lambda bi, i: (bi, i, 0, 0)),
            _const_spec((1, d)),
            _const_spec((qkv_rows, d)),
            _const_spec((d, 2 * conv_width)),
            _const_spec((QK_DIM, 1)),
            _const_spec((QK_DIM, 1)),
            _const_spec((HALF_ROT, 1)),
        ],
        out_specs=[
            pl.BlockSpec((1, 1, Q_COLS, tm), lambda bi, i: (bi, i, 0, 0)),
            pl.BlockSpec((1, tm, Q_COLS), lambda bi, i: (bi, i, 0)),
            pl.BlockSpec((1, 1, N_HEADS, V_AUG, tm), lambda bi, i: (bi, i, 0, 0, 0)),
            pl.BlockSpec((1, tm, conv_width), lambda bi, i: (bi, i, 0)),
        ],
        out_shape=[
            jax.ShapeDtypeStruct((b, nt, Q_COLS, tm), BF16),
            jax.ShapeDtypeStruct((b, s, Q_COLS), BF16),
            jax.ShapeDtypeStruct((b, nt, N_HEADS, V_AUG, tm), BF16),
            jax.ShapeDtypeStruct((b, s, conv_width), F32),
        ],
        compiler_params=pltpu.CompilerParams(
            dimension_semantics=("parallel", "parallel"), vmem_limit_bytes=VMEM_LIMIT_BYTES),
        name="in_proj",
    )(h, pos_f, g1, wqkv_t, wglu, qg, kg, invf)


def _attn_kernel(q_t_ref, k_ref, v_t_ref, lamv_ref, subg_ref, o_ref, acc_ref, sa_ref, sb_ref,
                 pa_ref, pb_ref, *, lam_init):
    t = k_ref.shape[1] // v_t_ref.shape[1]
    tq = Q_CHUNKS * t
    qi = pl.program_id(2)
    q_t = jnp.concatenate([q_t_ref[0, c] for c in range(Q_CHUNKS)], axis=1)
    zero_half = jnp.zeros((QK_DIM, tq), BF16)
    q_maps = (jnp.concatenate([q_t[:QK_DIM], zero_half], axis=0),
              jnp.concatenate([zero_half, q_t[QK_DIM:]], axis=0))

    def keys(j):
        return k_ref[0, pl.ds(pl.multiple_of(j * t, t), t), :]

    units = [(idx, slice(c * STRIP, (c + 1) * STRIP)) for c in range(tq // STRIP)
             for idx in range(2)]

    def score_unit(kc, s_ref, idx, cols):
        s = jnp.dot(kc, q_maps[idx][:, cols], preferred_element_type=F32)
        s_ref[idx, :, cols] = s
        return jnp.max(s, axis=0, keepdims=True)

    def numerator_unit(s_ref, p_ref, idx, cols, m_new):
        for r in range(2):
            rows = slice(r * (t // 2), (r + 1) * (t // 2))
            p_ref[idx, rows, cols] = jnp.exp2((s_ref[idx, rows, cols] - m_new).astype(BF16))

    def accumulate_unit(vc, p_ref, idx, cols, alpha):
        acc_ref[idx, :, cols] = alpha * acc_ref[idx, :, cols] + jnp.dot(
            vc, p_ref[idx, :, cols], preferred_element_type=F32)

    def stage(j_acc, p_acc_ref, alphas, s_cur_ref, p_cur_ref, ms, cmax, j_next, s_next_ref):
        vc = v_t_ref[0, j_acc, 0]
        kc = keys(j_next)
        new_ms, new_alphas, new_cmax = [], [], []
        for u, (idx, cols) in enumerate(units):
            new_cmax.append(score_unit(kc, s_next_ref, idx, cols))
            m_new = jnp.maximum(ms[u], cmax[u])
            numerator_unit(s_cur_ref, p_cur_ref, idx, cols, m_new)
            new_ms.append(m_new)
            new_alphas.append(jnp.exp2(ms[u] - m_new))
            accumulate_unit(vc, p_acc_ref, idx, cols, alphas[u])
        return tuple(new_ms), tuple(new_alphas), tuple(new_cmax)

    d0 = Q_CHUNKS * qi
    tri = (lax.broadcasted_iota(jnp.int32, (t, t), 0)
           <= lax.broadcasted_iota(jnp.int32, (t, t), 1))
    acc_ref[...] = jnp.zeros_like(acc_ref)
    kc_first = keys(0)
    kc = keys(d0 + 1)
    ms, alphas, cmax = [], [], []
    m_lo = jnp.full((1, STRIP), MASK_VALUE, F32)
    one = jnp.ones((1, STRIP), F32)
    for idx, cols in units:
        cmax.append(score_unit(kc_first, sa_ref, idx, cols))
        if cols.stop <= t:
            pb_ref[idx, :, cols] = jnp.zeros((t, STRIP), BF16)
            ms.append(m_lo)
        else:
            s = jnp.dot(kc, q_maps[idx][:, cols], preferred_element_type=F32)
            s = jnp.where(tri[:, cols.start - t:cols.stop - t], s, MASK_VALUE)
            m = jnp.max(s, axis=0, keepdims=True)
            pb_ref[idx, :, cols] = jnp.exp2((s - m).astype(BF16))
            ms.append(m)
        alphas.append(one)
    carry = (tuple(ms), tuple(alphas), tuple(cmax))

    def body(jj, carry):
        ms, alphas, cmax = carry
        c0 = Q_CHUNKS * jj
        ms, alphas, cmax = stage(jnp.where(jj == 0, d0 + 1, c0 - 1), pb_ref, alphas,
                                 sa_ref, pa_ref, ms, cmax, c0 + 1, sb_ref)
        return stage(c0, pa_ref, alphas, sb_ref, pb_ref, ms, cmax, c0 + 2, sa_ref)

    ms, alphas, _ = lax.fori_loop(0, qi, body, carry)

    vc_pending = v_t_ref[0, jnp.where(qi == 0, d0 + 1, d0 - 1), 0]
    vc = v_t_ref[0, d0, 0]
    for u, (idx, cols) in enumerate(units):
        accumulate_unit(vc_pending, pb_ref, idx, cols, alphas[u])
        s = sa_ref[idx, :, cols]
        if cols.stop <= t:
            s = jnp.where(tri[:, cols], s, MASK_VALUE)
        m_new = jnp.maximum(ms[u], jnp.max(s, axis=0, keepdims=True))
        pa_ref[idx, :, cols] = jnp.exp2((s - m_new).astype(BF16))
        accumulate_unit(vc, pa_ref, idx, cols, jnp.exp2(ms[u] - m_new))

    o_ref[0] = _attn_finalize(acc_ref, lamv_ref, subg_ref, lam_init)


def _attn_bounded_kernel(q_t_ref, k_ref, v_t_ref, lamv_ref, subg_ref, o_ref, acc_ref, pa_ref,
                         pb_ref, l_ref, *, lam_init):
    t = k_ref.shape[1] // v_t_ref.shape[1]
    tq = Q_CHUNKS * t
    qi = pl.program_id(2)
    q_t = jnp.concatenate([q_t_ref[0, c] for c in range(Q_CHUNKS)], axis=1)
    zero_half = jnp.zeros((QK_DIM, tq), BF16)
    q_maps = (jnp.concatenate([q_t[:QK_DIM], zero_half], axis=0),
              jnp.concatenate([zero_half, q_t[QK_DIM:]], axis=0))

    def keys(j):
        return k_ref[0, pl.ds(pl.multiple_of(j * t, t), t), :]

    units = [(idx, slice(c * STRIP, (c + 1) * STRIP)) for c in range(tq // STRIP)
             for idx in range(2)]
    tri = (lax.broadcasted_iota(jnp.int32, (t, t), 0)
           <= lax.broadcasted_iota(jnp.int32, (t, t), 1))

    def numerator_unit(kc, p_ref, idx, cols, mask=None):
        s = jnp.dot(kc, q_maps[idx][:, cols], preferred_element_type=F32)
        if mask is not None:
            s = jnp.where(mask, s, MASK_VALUE)
        p = jnp.exp2(s)
        p_ref[idx, :, cols] = p.astype(BF16)
        l_ref[idx, :, cols] += jnp.sum(p.reshape(t // SUBLANES, SUBLANES, STRIP), axis=0)

    def accumulate_unit(vc, p_ref, idx, cols):
        acc_ref[idx, :V_DIM, cols] += jnp.dot(vc[:V_DIM], p_ref[idx, :, cols],
                                              preferred_element_type=F32)

    def stage(j_acc, p_acc_ref, j_new, p_new_ref):
        vc = v_t_ref[0, j_acc, 0]
        kc = keys(j_new)
        for idx, cols in units:
            numerator_unit(kc, p_new_ref, idx, cols)
            accumulate_unit(vc, p_acc_ref, idx, cols)

    d0 = Q_CHUNKS * qi
    acc_ref[...] = jnp.zeros_like(acc_ref)
    l_ref[...] = jnp.zeros_like(l_ref)
    kc_hi, kc_lo = keys(d0 + 1), keys(d0)
    vc_hi = v_t_ref[0, d0 + 1, 0]
    for idx, cols in units:
        if cols.stop <= t:
            numerator_unit(kc_lo, pa_ref, idx, cols, tri[:, cols])
        else:
            numerator_unit(kc_lo, pa_ref, idx, cols)
            numerator_unit(kc_hi, pb_ref, idx, cols, tri[:, cols.start - t:cols.stop - t])
            accumulate_unit(vc_hi, pb_ref, idx, cols)

    def body(jj, carry):
        c0 = Q_CHUNKS * jj
        stage(jnp.where(jj == 0, d0, c0 - 1), pa_ref, c0, pb_ref)
        stage(c0, pb_ref, c0 + 1, pa_ref)
        return carry

    lax.fori_loop(0, qi, body, 0)
    vc = v_t_ref[0, jnp.where(qi == 0, d0, d0 - 1), 0]
    for idx, cols in units:
        accumulate_unit(vc, pa_ref, idx, cols)

    for idx in range(2):
        acc_ref[idx, V_DIM:V_DIM + 1] = jnp.sum(l_ref[idx], axis=0, keepdims=True)
    o_ref[0] = _attn_finalize(acc_ref, lamv_ref, subg_ref, lam_init)


def _attn_finalize(acc_ref, lamv_ref, subg_ref, lam_init):
    lamv = lamv_ref[...]
    lam = (jnp.exp(jnp.sum(lamv[0:1] * lamv[1:2], axis=-1, keepdims=True))
           - jnp.exp(jnp.sum(lamv[2:3] * lamv[3:4], axis=-1, keepdims=True)) + lam_init)
    o1 = acc_ref[0, :V_DIM] * (1.0 / acc_ref[0, V_DIM:V_DIM + 1])
    o2 = acc_ref[1, :V_DIM] * (1.0 / acc_ref[1, V_DIM:V_DIM + 1])
    o = o1 - lam * o2
    ms_o = jnp.mean(o * o, axis=0, keepdims=True)
    y = o * lax.rsqrt(ms_o + NORM_EPS) * subg_ref[...] * (1.0 - lam_init)
    return y.T.astype(BF16)


def _attn(q_t, k, v_t, lamv, subg, lam_init, bounded):
    b, nt, _, t = q_t.shape
    s = k.shape[1]
    tq = Q_CHUNKS * t
    p_bufs = [pltpu.VMEM((2, t, tq), BF16), pltpu.VMEM((2, t, tq), BF16)]
    s_bufs = [] if bounded else [pltpu.VMEM((2, t, tq), F32), pltpu.VMEM((2, t, tq), F32)]
    body = _attn_bounded_kernel if bounded else _attn_kernel
    return pl.pallas_call(
        functools.partial(body, lam_init=lam_init),
        grid=(b, N_HEADS, s // tq),
        in_specs=[
            pl.BlockSpec((1, Q_CHUNKS, 2 * QK_DIM, t), lambda bi, hi, qi: (bi, qi, hi, 0)),
            pl.BlockSpec((1, s, 2 * QK_DIM), lambda bi, hi, qi: (bi, 0, hi)),
            pl.BlockSpec((1, nt, 1, V_AUG, t), lambda bi, hi, qi: (bi, 0, hi, 0, 0)),
            _const_spec((4, QK_DIM)),
            _const_spec((V_DIM, 1)),
        ],
        out_specs=pl.BlockSpec((1, tq, V_DIM), lambda bi, hi, qi: (bi, qi, hi)),
        out_shape=jax.ShapeDtypeStruct((b, s, ATTN_WIDTH), BF16),
        scratch_shapes=([pltpu.VMEM((2, V_AUG, tq), F32)] + s_bufs + p_bufs
                        + ([pltpu.VMEM((2, SUBLANES, tq), F32)] if bounded else [])),
        compiler_params=pltpu.CompilerParams(
            dimension_semantics=("parallel", "parallel", "arbitrary"),
            vmem_limit_bytes=VMEM_LIMIT_BYTES),
        name="attn_bounded" if bounded else "attn",
    )(q_t, k, v_t, lamv, subg)


def _mix_kernel(h_ref, attn_ref, c_ref, halo_ref, cw_ref, cb_ref, lng_ref, lnb_ref,
                wo_attn_ref, wo_conv_ref, o_ref, buf_ref, shift_ref):
    tm = c_ref.shape[1]
    taps = cw_ref.shape[0]
    first = pl.program_id(1) == 0
    halo = halo_ref[0]
    buf_ref[:CONV_HALO] = jnp.where(first, jnp.zeros_like(halo), halo)
    buf_ref[CONV_HALO:] = c_ref[0]
    base = CONV_HALO - (taps - 1)
    span = shift_ref.shape[1]
    for r in range(1, SUBLANES):
        shift_ref[r] = buf_ref[pl.ds(r, span), :]
    cw = cw_ref[...]
    y = cb_ref[...]
    for j in range(taps):
        r, a = (base + j) % SUBLANES, (base + j) // SUBLANES
        src = buf_ref if r == 0 else shift_ref.at[r]
        y = y + cw[j:j + 1] * src[pl.ds(a * SUBLANES, tm), :]
    mu = jnp.mean(y, axis=-1, keepdims=True)
    yc = y - mu
    z = yc * lax.rsqrt(jnp.mean(yc * yc, axis=-1, keepdims=True) + NORM_EPS)
    z = z * lng_ref[...] + lnb_ref[...]
    z = (z * jax.nn.sigmoid(z)).astype(BF16)
    upd = (jnp.dot(attn_ref[0], wo_attn_ref[...], preferred_element_type=F32)
           + jnp.dot(z, wo_conv_ref[...], preferred_element_type=F32))
    o_ref[0] = h_ref[0] + upd


def _mix(h, attn, c, cw, cb, lng, lnb, wo_attn, wo_conv):
    b, s, d = h.shape
    tm = ROW_TILE
    cwid = c.shape[-1]
    halo_blocks = tm // CONV_HALO
    return pl.pallas_call(
        _mix_kernel,
        grid=(b, s // tm),
        in_specs=[
            pl.BlockSpec((1, tm, d), lambda bi, i: (bi, i, 0)),
            pl.BlockSpec((1, tm, ATTN_WIDTH), lambda bi, i: (bi, i, 0)),
            pl.BlockSpec((1, tm, cwid), lambda bi, i: (bi, i, 0)),
            pl.BlockSpec((1, CONV_HALO, cwid),
                         lambda bi, i: (bi, jnp.maximum(i * halo_blocks - 1, 0), 0)),
            _const_spec(cw.shape),
            _const_spec((1, cwid)),
            _const_spec((1, cwid)),
            _const_spec((1, cwid)),
            _const_spec(wo_attn.shape),
            _const_spec(wo_conv.shape),
        ],
        out_specs=pl.BlockSpec((1, tm, d), lambda bi, i: (bi, i, 0)),
        out_shape=jax.ShapeDtypeStruct((b, s, d), F32),
        scratch_shapes=[pltpu.VMEM((CONV_HALO + tm, cwid), F32),
                        pltpu.VMEM((SUBLANES, CONV_HALO + tm - SUBLANES, cwid), F32)],
        compiler_params=pltpu.CompilerParams(
            dimension_semantics=("parallel", "parallel"), vmem_limit_bytes=VMEM_LIMIT_BYTES),
        name="mix",
    )(h, attn, c, c, cw, cb, lng, lnb, wo_attn, wo_conv)


def _ffn_kernel(h_ref, halo_ref, p_ref, g2_ref, wa_ref, wb_ref, fwa_ref, fwb_ref, fba_ref,
                fbb_ref, wd_ref, pg_ref, wg_ref, bg_ref, wp_ref, o_ref):
    tm = h_ref.shape[1]
    first = pl.program_id(1) == 0
    h = h_ref[0]
    halo = halo_ref[0]
    halo = jnp.where(first, jnp.zeros_like(halo), halo)
    x = jnp.concatenate([halo, h], axis=0)
    xn = _rms_rows(x, g2_ref[...]).astype(BF16)

    def conv_up(w_ref, cw_ref, cb_ref):
        u = jnp.dot(xn, w_ref[...], preferred_element_type=F32)
        cw = cw_ref[...]
        y = cb_ref[...] + cw[2:3] * u[FFN_HALO:]
        y = y + cw[1:2] * u[FFN_HALO - 1:FFN_HALO - 1 + tm]
        y = y + cw[0:1] * u[FFN_HALO - 2:FFN_HALO - 2 + tm]
        return y

    ya = conv_up(wa_ref, fwa_ref, fba_ref)
    yb = conv_up(wb_ref, fwb_ref, fbb_ref)
    act = (ya * jax.nn.sigmoid(ya) * yb).astype(BF16)
    h = h + jnp.dot(act, wd_ref[...], preferred_element_type=F32)

    hn = _rms_rows(h, pg_ref[...]).astype(BF16)
    gate = jax.nn.sigmoid(jnp.dot(hn, wg_ref[...], preferred_element_type=F32) + bg_ref[...])
    ple = jnp.dot(p_ref[0].astype(BF16), wp_ref[...], preferred_element_type=F32)
    o_ref[0] = h + gate * ple


def _ffn(h, p, g2, wa, wb, fwa, fwb, fba, fbb, wd, pg, wg, bg, wp):
    b, s, d = h.shape
    tm = ROW_TILE
    halo_blocks = tm // FFN_HALO
    consts = (g2, wa, wb, fwa, fwb, fba, fbb, wd, pg, wg, bg, wp)
    return pl.pallas_call(
        _ffn_kernel,
        grid=(b, s // tm),
        in_specs=[
            pl.BlockSpec((1, tm, d), lambda bi, i: (bi, i, 0)),
            pl.BlockSpec((1, FFN_HALO, d),
                         lambda bi, i: (bi, jnp.maximum(i * halo_blocks - 1, 0), 0)),
            pl.BlockSpec((1, tm, p.shape[-1]), lambda bi, i: (bi, i, 0)),
        ] + [_const_spec(a.shape) for a in consts],
        out_specs=pl.BlockSpec((1, tm, d), lambda bi, i: (bi, i, 0)),
        out_shape=jax.ShapeDtypeStruct((b, s, d), F32),
        compiler_params=pltpu.CompilerParams(
            dimension_semantics=("parallel", "parallel"), vmem_limit_bytes=VMEM_LIMIT_BYTES),
        name="ffn",
    )(h, h, p, *consts)


def _pad_cols(a, width):
    return jnp.pad(a, ((0, 0), (0, width - a.shape[1])))


def kernel(x, p, positions, norm1_g, w_in, q_norm_g, k_norm_g, lam_q1, lam_k1, lam_q2, lam_k2,
           subln_g, conv_w, conv_b, conv_ln_g, conv_ln_b, w_out, norm2_g, w_up, ffn_conv_w,
           ffn_conv_b, w_down, ple_norm_g, w_ple_gate, b_ple_gate, w_ple):
    b, s, d = x.shape
    depth = w_in.shape[0]
    assert s % (Q_CHUNKS * ROW_TILE) == 0 and d % LANES == 0
    d_ff = w_down.shape[1]
    d_ff_pad = -(-d_ff // LANES) * LANES
    nt = s // ROW_TILE

    pos_f = positions.astype(F32).reshape(b, nt, 1, ROW_TILE)
    invf = (ROPE_THETA ** (-jnp.arange(0, ROT_DIM, 2, dtype=F32) / ROT_DIM)).reshape(HALF_ROT, 1)
    row = lambda v: v.reshape(1, -1).astype(F32)
    col = lambda v: v.reshape(-1, 1).astype(F32)

    h = x
    for i in range(depth):
        lam_init = 0.8 - 0.6 * math.exp(-0.3 * i)
        qkv_cols = 2 * Q_COLS + ATTN_WIDTH
        wqkv_t = w_in[i][:, :qkv_cols].T.astype(BF16)
        wglu = w_in[i][:, qkv_cols:].astype(BF16)
        q_t, k, v_t, c = _in_proj(h, pos_f, row(norm1_g[i]), wqkv_t, wglu,
                                  col(q_norm_g[i]), col(k_norm_g[i]), invf)

        lamv = jnp.stack([lam_q1[i], lam_k1[i], lam_q2[i], lam_k2[i]]).astype(F32)
        bound = (1.01 * QK_DIM ** 0.5 * LOG2_E) * (jnp.max(jnp.abs(q_norm_g[i]))
                                                   * jnp.max(jnp.abs(k_norm_g[i])))
        attn_args = (q_t, k, v_t, lamv, col(subln_g[i]))
        attn = lax.cond(bound <= SCORE_BOUND,
                        lambda *a: _attn(*a, lam_init, True),
                        lambda *a: _attn(*a, lam_init, False), *attn_args)

        wo = w_out[i].astype(BF16)
        h = _mix(h, attn, c, conv_w[i].astype(F32), row(conv_b[i]), row(conv_ln_g[i]),
                 row(conv_ln_b[i]), wo[:ATTN_WIDTH], wo[ATTN_WIDTH:])

        wa = _pad_cols(w_up[i][:, :d_ff], d_ff_pad).astype(BF16)
        wb = _pad_cols(w_up[i][:, d_ff:], d_ff_pad).astype(BF16)
        fwa = _pad_cols(ffn_conv_w[i][:, :d_ff], d_ff_pad).astype(F32)
        fwb = _pad_cols(ffn_conv_w[i][:, d_ff:], d_ff_pad).astype(F32)
        fba = _pad_cols(row(ffn_conv_b[i][:d_ff]), d_ff_pad)
        fbb = _pad_cols(row(ffn_conv_b[i][d_ff:]), d_ff_pad)
        wd = jnp.pad(w_down[i], ((0, d_ff_pad - d_ff), (0, 0))).astype(BF16)
        h = _ffn(h, p[i], row(norm2_g[i]), wa, wb, fwa, fwb, fba, fbb, wd,
                 row(ple_norm_g[i]), w_ple_gate[i].astype(BF16), row(b_ple_gate[i]),
                 w_ple[i].astype(BF16))
    return h
```

```python
import functools
import math

import jax
import jax.numpy as jnp
from jax import lax
from jax.experimental import pallas as pl
from jax.experimental.pallas import tpu as pltpu

N_HEADS = 4
QK_DIM = 64
V_DIM = 2 * QK_DIM
ATTN_WIDTH = N_HEADS * V_DIM
Q_COLS = N_HEADS * 2 * QK_DIM
ROT_DIM = QK_DIM // 4
HALF_ROT = ROT_DIM // 2
ROPE_THETA = 500000.0
NORM_EPS = 1e-6
MASK_VALUE = -1e30
LOG2_E = math.log2(math.e)
V_AUG = V_DIM + 16

LANES = 128
SUBLANES = 8
VMEM_LIMIT_BYTES = 56 * 1024 * 1024

ROW_TILE = 512
Q_CHUNKS = 2
STRIP = 512
SCORE_BOUND = 64.0
CONV_HALO = 32
FFN_HALO = 8

F32 = jnp.float32
BF16 = jnp.bfloat16


def _const_spec(shape):
    return pl.BlockSpec(shape, lambda *_: (0,) * len(shape), pipeline_mode=pl.Buffered(1))


def _rms_rows(x, g):
    ms = jnp.mean(x * x, axis=-1, keepdims=True)
    return x * lax.rsqrt(ms + NORM_EPS) * g


def _in_proj_kernel(h_ref, pos_ref, g1_ref, wqkv_t_ref, wglu_ref, qg_ref, kg_ref, invf_ref,
                    q_t_ref, k_ref, v_t_ref, c_ref, *, conv_width):
    hn = _rms_rows(h_ref[0], g1_ref[...]).astype(BF16)
    u_t = lax.dot_general(wqkv_t_ref[...], hn, (((1,), (1,)), ((), ())),
                          preferred_element_type=F32)
    glu = jnp.dot(hn, wglu_ref[...], preferred_element_type=F32)
    c_ref[0] = glu[:, :conv_width] * jax.nn.sigmoid(glu[:, conv_width:])

    ang = invf_ref[...] * pos_ref[0, 0]
    cos, sin = jnp.cos(ang), jnp.sin(ang)

    def norm_rope(x_t, g):
        segs = []
        for s in range(Q_COLS // QK_DIM):
            seg = x_t[s * QK_DIM:(s + 1) * QK_DIM]
            ms = jnp.mean(seg * seg, axis=0, keepdims=True)
            y = seg * lax.rsqrt(ms + NORM_EPS) * g
            x1, x2 = y[:HALF_ROT], y[HALF_ROT:ROT_DIM]
            segs += [x1 * cos - x2 * sin, x2 * cos + x1 * sin, y[ROT_DIM:]]
        return jnp.concatenate(segs, axis=0)

    q_t = norm_rope(u_t[:Q_COLS], qg_ref[...]) * (QK_DIM ** -0.5 * LOG2_E)
    k_t = norm_rope(u_t[Q_COLS:2 * Q_COLS], kg_ref[...])
    q_t_ref[0, 0] = q_t.astype(BF16)
    k_ref[0] = k_t.T.astype(BF16)
    ones = jnp.ones((V_AUG - V_DIM, u_t.shape[1]), BF16)
    for hd in range(N_HEADS):
        v_rows = u_t[2 * Q_COLS + hd * V_DIM:2 * Q_COLS + (hd + 1) * V_DIM]
        v_t_ref[0, 0, hd, :V_DIM] = v_rows.astype(BF16)
        v_t_ref[0, 0, hd, V_DIM:] = ones


def _in_proj(h, pos_f, g1, wqkv_t, wglu, qg, kg, invf):
    b, s, d = h.shape
    tm = ROW_TILE
    nt = s // tm
    conv_width = wglu.shape[1] // 2
    qkv_rows = wqkv_t.shape[0]
    return pl.pallas_call(
        functools.partial(_in_proj_kernel, conv_width=conv_width),
        grid=(b, nt),
        in_specs=[
            pl.BlockSpec((1, tm, d), lambda bi, i: (bi, i, 0)),
            pl.BlockSpec((1, 1, 1, tm), lambda bi, i: (bi, i, 0, 0)),
            _const_spec((1, d)),
            _const_spec((qkv_rows, d)),
            _const_spec((d, 2 * conv_width)),
            _const_spec((QK_DIM, 1)),
            _const_spec((QK_DIM, 1)),
            _const_spec((HALF_ROT, 1)),
        ],
        out_specs=[
            pl.BlockSpec((1, 1, Q_COLS, tm), lambda bi, i: (bi, i, 0, 0)),
            pl.BlockSpec((1, tm, Q_COLS), lambda bi, i: (bi, i, 0)),
            pl.BlockSpec((1, 1, N_HEADS, V_AUG, tm), lambda bi, i: (bi, i, 0, 0, 0)),
            pl.BlockSpec((1, tm, conv_width), lambda bi, i: (bi, i, 0)),
        ],
        out_shape=[
            jax.ShapeDtypeStruct((b, nt, Q_COLS, tm), BF16),
            jax.ShapeDtypeStruct((b, s, Q_COLS), BF16),
            jax.ShapeDtypeStruct((b, nt, N_HEADS, V_AUG, tm), BF16),
            jax.ShapeDtypeStruct((b, s, conv_width), F32),
        ],
        compiler_params=pltpu.CompilerParams(
            dimension_semantics=("parallel", "parallel"), vmem_limit_bytes=VMEM_LIMIT_BYTES),
        name="in_proj",
    )(h, pos_f, g1, wqkv_t, wglu, qg, kg, invf)


def _attn_kernel(q_t_ref, k_ref, v_t_ref, lamv_ref, subg_ref, o_ref, acc_ref, sa_ref, sb_ref,
                 pa_ref, pb_ref, *, lam_init):
    t = k_ref.shape[1] // v_t_ref.shape[1]
    tq = Q_CHUNKS * t
    qi = pl.program_id(2)
    q_t = jnp.concatenate([q_t_ref[0, c] for c in range(Q_CHUNKS)], axis=1)
    zero_half = jnp.zeros((QK_DIM, tq), BF16)
    q_maps = (jnp.concatenate([q_t[:QK_DIM], zero_half], axis=0),
              jnp.concatenate([zero_half, q_t[QK_DIM:]], axis=0))

    def keys(j):
        return k_ref[0, pl.ds(pl.multiple_of(j * t, t), t), :]

    units = [(idx, slice(c * STRIP, (c + 1) * STRIP)) for c in range(tq // STRIP)
             for idx in range(2)]

    def score_unit(kc, s_ref, idx, cols):
        s = jnp.dot(kc, q_maps[idx][:, cols], preferred_element_type=F32)
        s_ref[idx, :, cols] = s
        return jnp.max(s, axis=0, keepdims=True)

    def numerator_unit(s_ref, p_ref, idx, cols, m_new):
        for r in range(2):
            rows = slice(r * (t // 2), (r + 1) * (t // 2))
            p_ref[idx, rows, cols] = jnp.exp2((s_ref[idx, rows, cols] - m_new).astype(BF16))

    def accumulate_unit(vc, p_ref, idx, cols, alpha):
        acc_ref[idx, :, cols] = alpha * acc_ref[idx, :, cols] + jnp.dot(
            vc, p_ref[idx, :, cols], preferred_element_type=F32)

    def stage(j_acc, p_acc_ref, alphas, s_cur_ref, p_cur_ref, ms, cmax, j_next, s_next_ref):
        vc = v_t_ref[0, j_acc, 0]
        kc = keys(j_next)
        new_ms, new_alphas, new_cmax = [], [], []
        for u, (idx, cols) in enumerate(units):
            new_cmax.append(score_unit(kc, s_next_ref, idx, cols))
            m_new = jnp.maximum(ms[u], cmax[u])
            numerator_unit(s_cur_ref, p_cur_ref, idx, cols, m_new)
            new_ms.append(m_new)
            new_alphas.append(jnp.exp2(ms[u] - m_new))
            accumulate_unit(vc, p_acc_ref, idx, cols, alphas[u])
        return tuple(new_ms), tuple(new_alphas), tuple(new_cmax)

    d0 = Q_CHUNKS * qi
    tri = (lax.broadcasted_iota(jnp.int32, (t, t), 0)
           <= lax.broadcasted_iota(jnp.int32, (t, t), 1))
    acc_ref[...] = jnp.zeros_like(acc_ref)
    kc_first = keys(0)
    kc = keys(d0 + 1)
    ms, alphas, cmax = [], [], []
    m_lo = jnp.full((1, STRIP), MASK_VALUE, F32)
    one = jnp.ones((1, STRIP), F32)
    for idx, cols in units:
        cmax.append(score_unit(kc_first, sa_ref, idx, cols))
        if cols.stop <= t:
            pb_ref[idx, :, cols] = jnp.zeros((t, STRIP), BF16)
            ms.append(m_lo)
        else:
            s = jnp.dot(kc, q_maps[idx][:, cols], preferred_element_type=F32)
            s = jnp.where(tri[:, cols.start - t:cols.stop - t], s, MASK_VALUE)
            m = jnp.max(s, axis=0, keepdims=True)
            pb_ref[idx, :, cols] = jnp.exp2((s - m).astype(BF16))
            ms.append(m)
        alphas.append(one)
    carry = (tuple(ms), tuple(alphas), tuple(cmax))

    def body(jj, carry):
        ms, alphas, cmax = carry
        c0 = Q_CHUNKS * jj
        ms, alphas, cmax = stage(jnp.where(jj == 0, d0 + 1, c0 - 1), pb_ref, alphas,
                                 sa_ref, pa_ref, ms, cmax, c0 + 1, sb_ref)
        return stage(c0, pa_ref, alphas, sb_ref, pb_ref, ms, cmax, c0 + 2, sa_ref)

    ms, alphas, _ = lax.fori_loop(0, qi, body, carry)

    vc_pending = v_t_ref[0, jnp.where(qi == 0, d0 + 1, d0 - 1), 0]
    vc = v_t_ref[0, d0, 0]
    for u, (idx, cols) in enumerate(units):
        accumulate_unit(vc_pending, pb_ref, idx, cols, alphas[u])
        s = sa_ref[idx, :, cols]
        if cols.stop <= t:
            s = jnp.where(tri[:, cols], s, MASK_VALUE)
        m_new = jnp.maximum(ms[u], jnp.max(s, axis=0, keepdims=True))
        pa_ref[idx, :, cols] = jnp.exp2((s - m_new).astype(BF16))
        accumulate_unit(vc, pa_ref, idx, cols, jnp.exp2(ms[u] - m_new))

    o_ref[0] = _attn_finalize(acc_ref, lamv_ref, subg_ref, lam_init)


def _attn_bounded_kernel(q_t_ref, k_ref, v_t_ref, lamv_ref, subg_ref, o_ref, acc_ref, pa_ref,
                         pb_ref, l_ref, *, lam_init):
    t = k_ref.shape[1] // v_t_ref.shape[1]
    tq = Q_CHUNKS * t
    qi = pl.program_id(2)
    q_t = jnp.concatenate([q_t_ref[0, c] for c in range(Q_CHUNKS)], axis=1)
    zero_half = jnp.zeros((QK_DIM, tq), BF16)
    q_maps = (jnp.concatenate([q_t[:QK_DIM], zero_half], axis=0),
              jnp.concatenate([zero_half, q_t[QK_DIM:]], axis=0))

    def keys(j):
        return k_ref[0, pl.ds(pl.multiple_of(j * t, t), t), :]

    units = [(idx, slice(c * STRIP, (c + 1) * STRIP)) for c in range(tq // STRIP)
             for idx in range(2)]
    tri = (lax.broadcasted_iota(jnp.int32, (t, t), 0)
           <= lax.broadcasted_iota(jnp.int32, (t, t), 1))

    def numerator_unit(kc, p_ref, idx, cols, mask=None):
        s = jnp.dot(kc, q_maps[idx][:, cols], preferred_element_type=F32)
        if mask is not None:
            s = jnp.where(mask, s, MASK_VALUE)
        p = jnp.exp2(s)
        p_ref[idx, :, cols] = p.astype(BF16)
        l_ref[idx, :, cols] += jnp.sum(p.reshape(t // SUBLANES, SUBLANES, STRIP), axis=0)

    def accumulate_unit(vc, p_ref, idx, cols):
        acc_ref[idx, :V_DIM, cols] += jnp.dot(vc[:V_DIM], p_ref[idx, :, cols],
                                              preferred_element_type=F32)

    def stage(j_acc, p_acc_ref, j_new, p_new_ref):
        vc = v_t_ref[0, j_acc, 0]
        kc = keys(j_new)
        for idx, cols in units:
            numerator_unit(kc, p_new_ref, idx, cols)
            accumulate_unit(vc, p_acc_ref, idx, cols)

    d0 = Q_CHUNKS * qi
    acc_ref[...] = jnp.zeros_like(acc_ref)
    l_ref[...] = jnp.zeros_like(l_ref)
    kc_hi, kc_lo = keys(d0 + 1), keys(d0)
    vc_hi = v_t_ref[0, d0 + 1, 0]
    for idx, cols in units:
        if cols.stop <= t:
            numerator_unit(kc_lo, pa_ref, idx, cols, tri[:, cols])
        else:
            numerator_unit(kc_lo, pa_ref, idx, cols)
            numerator_unit(kc_hi, pb_ref, idx, cols, tri[:, cols.start - t:cols.stop - t])
            accumulate_unit(vc_hi, pb_ref, idx, cols)

    def chunk_pair(jj):
        c0 = Q_CHUNKS * jj
        stage(jnp.where(jj == 0, d0, c0 - 1), pa_ref, c0, pb_ref)
        stage(c0, pb_ref, c0 + 1, pa_ref)

    def body(jj, carry):
        chunk_pair(2 * jj)
        chunk_pair(2 * jj + 1)
        return carry

    lax.fori_loop(0, qi // 2, body, 0)

    @pl.when(qi % 2 == 1)
    def _():
        chunk_pair(qi - 1)

    vc = v_t_ref[0, jnp.where(qi == 0, d0, d0 - 1), 0]
    for idx, cols in units:
        accumulate_unit(vc, pa_ref, idx, cols)

    for idx in range(2):
        acc_ref[idx, V_DIM:V_DIM + 1] = jnp.sum(l_ref[idx], axis=0, keepdims=True)
    o_ref[0] = _attn_finalize(acc_ref, lamv_ref, subg_ref, lam_init)


def _attn_finalize(acc_ref, lamv_ref, subg_ref, lam_init):
    lamv = lamv_ref[...]
    lam = (jnp.exp(jnp.sum(lamv[0:1] * lamv[1:2], axis=-1, keepdims=True))
           - jnp.exp(jnp.sum(lamv[2:3] * lamv[3:4], axis=-1, keepdims=True)) + lam_init)
    o1 = acc_ref[0, :V_DIM] * (1.0 / acc_ref[0, V_DIM:V_DIM + 1])
    o2 = acc_ref[1, :V_DIM] * (1.0 / acc_ref[1, V_DIM:V_DIM + 1])
    o = o1 - lam * o2
    ms_o = jnp.mean(o * o, axis=0, keepdims=True)
    y = o * lax.rsqrt(ms_o + NORM_EPS) * subg_ref[...] * (1.0 - lam_init)
    return y.T.astype(BF16)


def _attn(q_t, k, v_t, lamv, subg, lam_init, bounded):
    b, nt, _, t = q_t.shape
    s = k.shape[1]
    tq = Q_CHUNKS * t
    p_bufs = [pltpu.VMEM((2, t, tq), BF16), pltpu.VMEM((2, t, tq), BF16)]
    s_bufs = [] if bounded else [pltpu.VMEM((2, t, tq), F32), pltpu.VMEM((2, t, tq), F32)]
    body = _attn_bounded_kernel if bounded else _attn_kernel
    return pl.pallas_call(
        functools.partial(body, lam_init=lam_init),
        grid=(b, N_HEADS, s // tq),
        in_specs=[
            pl.BlockSpec((1, Q_CHUNKS, 2 * QK_DIM, t), lambda bi, hi, qi: (bi, qi, hi, 0)),
            pl.BlockSpec((1, s, 2 * QK_DIM), lambda bi, hi, qi: (bi, 0, hi)),
            pl.BlockSpec((1, nt, 1, V_AUG, t), lambda bi, hi, qi: (bi, 0, hi, 0, 0)),
            _const_spec((4, QK_DIM)),
            _const_spec((V_DIM, 1)),
        ],
        out_specs=pl.BlockSpec((1, tq, V_DIM), lambda bi, hi, qi: (bi, qi, hi)),
        out_shape=jax.ShapeDtypeStruct((b, s, ATTN_WIDTH), BF16),
        scratch_shapes=([pltpu.VMEM((2, V_AUG, tq), F32)] + s_bufs + p_bufs
                        + ([pltpu.VMEM((2, SUBLANES, tq), F32)] if bounded else [])),
        compiler_params=pltpu.CompilerParams(
            dimension_semantics=("parallel", "parallel", "arbitrary"),
            vmem_limit_bytes=VMEM_LIMIT_BYTES),
        name="attn_bounded" if bounded else "attn",
    )(q_t, k, v_t, lamv, subg)


def _mix_kernel(h_ref, attn_ref, c_ref, halo_ref, cw_ref, cb_ref, lng_ref, lnb_ref,
                wo_attn_ref, wo_conv_ref, o_ref, buf_ref, shift_ref):
    tm = c_ref.shape[1]
    taps = cw_ref.shape[0]
    first = pl.program_id(1) == 0
    halo = halo_ref[0]
    buf_ref[:CONV_HALO] = jnp.where(first, jnp.zeros_like(halo), halo)
    buf_ref[CONV_HALO:] = c_ref[0]
    base = CONV_HALO - (taps - 1)
    span = shift_ref.shape[1]
    for r in range(1, SUBLANES):
        shift_ref[r] = buf_ref[pl.ds(r, span), :]
    cw = cw_ref[...]
    y = cb_ref[...]
    for j in range(taps):
        r, a = (base + j) % SUBLANES, (base + j) // SUBLANES
        src = buf_ref if r == 0 else shift_ref.at[r]
        y = y + cw[j:j + 1] * src[pl.ds(a * SUBLANES, tm), :]
    mu = jnp.mean(y, axis=-1, keepdims=True)
    yc = y - mu
    z = yc * lax.rsqrt(jnp.mean(yc * yc, axis=-1, keepdims=True) + NORM_EPS)
    z = z * lng_ref[...] + lnb_ref[...]
    z = (z * jax.nn.sigmoid(z)).astype(BF16)
    upd = (jnp.dot(attn_ref[0], wo_attn_ref[...], preferred_element_type=F32)
           + jnp.dot(z, wo_conv_ref[...], preferred_element_type=F32))
    o_ref[0] = h_ref[0] + upd


def _mix(h, attn, c, cw, cb, lng, lnb, wo_attn, wo_conv):
    b, s, d = h.shape
    tm = ROW_TILE
    cwid = c.shape[-1]
    halo_blocks = tm // CONV_HALO
    return pl.pallas_call(
        _mix_kernel,
        grid=(b, s // tm),
        in_specs=[
            pl.BlockSpec((1, tm, d), lambda bi, i: (bi, i, 0)),
            pl.BlockSpec((1, tm, ATTN_WIDTH), lambda bi, i: (bi, i, 0)),
            pl.BlockSpec((1, tm, cwid), lambda bi, i: (bi, i, 0)),
            pl.BlockSpec((1, CONV_HALO, cwid),
                         lambda bi, i: (bi, jnp.maximum(i * halo_blocks - 1, 0), 0)),
            _const_spec(cw.shape),
            _const_spec((1, cwid)),
            _const_spec((1, cwid)),
            _const_spec((1, cwid)),
            _const_spec(wo_attn.shape),
            _const_spec(wo_conv.shape),
        ],
        out_specs=pl.BlockSpec((1, tm, d), lambda bi, i: (bi, i, 0)),
        out_shape=jax.ShapeDtypeStruct((b, s, d), F32),
        scratch_shapes=[pltpu.VMEM((CONV_HALO + tm, cwid), F32),
                        pltpu.VMEM((SUBLANES, CONV_HALO + tm - SUBLANES, cwid), F32)],
        compiler_params=pltpu.CompilerParams(
            dimension_semantics=("parallel", "parallel"), vmem_limit_bytes=VMEM_LIMIT_BYTES),
        name="mix",
    )(h, attn, c, c, cw, cb, lng, lnb, wo_attn, wo_conv)


def _ffn_kernel(h_ref, halo_ref, p_ref, g2_ref, wa_ref, wb_ref, fwa_ref, fwb_ref, fba_ref,
                fbb_ref, wd_ref, pg_ref, wg_ref, bg_ref, wp_ref, o_ref):
    tm = h_ref.shape[1]
    first = pl.program_id(1) == 0
    h = h_ref[0]
    halo = halo_ref[0]
    halo = jnp.where(first, jnp.zeros_like(halo), halo)
    x = jnp.concatenate([halo, h], axis=0)
    xn = _rms_rows(x, g2_ref[...]).astype(BF16)

    def conv_up(w_ref, cw_ref, cb_ref):
        u = jnp.dot(xn, w_ref[...], preferred_element_type=F32)
        cw = cw_ref[...]
        y = cb_ref[...] + cw[2:3] * u[FFN_HALO:]
        y = y + cw[1:2] * u[FFN_HALO - 1:FFN_HALO - 1 + tm]
        y = y + cw[0:1] * u[FFN_HALO - 2:FFN_HALO - 2 + tm]
        return y

    ya = conv_up(wa_ref, fwa_ref, fba_ref)
    yb = conv_up(wb_ref, fwb_ref, fbb_ref)
    act = (ya * jax.nn.sigmoid(ya) * yb).astype(BF16)
    h = h + jnp.dot(act, wd_ref[...], preferred_element_type=F32)

    hn = _rms_rows(h, pg_ref[...]).astype(BF16)
    gate = jax.nn.sigmoid(jnp.dot(hn, wg_ref[...], preferred_element_type=F32) + bg_ref[...])
    ple = jnp.dot(p_ref[0].astype(BF16), wp_ref[...], preferred_element_type=F32)
    o_ref[0] = h + gate * ple


def _ffn(h, p, g2, wa, wb, fwa, fwb, fba, fbb, wd, pg, wg, bg, wp):
    b, s, d = h.shape
    tm = ROW_TILE
    halo_blocks = tm // FFN_HALO
    consts = (g2, wa, wb, fwa, fwb, fba, fbb, wd, pg, wg, bg, wp)
    return pl.pallas_call(
        _ffn_kernel,
        grid=(b, s // tm),
        in_specs=[
            pl.BlockSpec((1, tm, d), lambda bi, i: (bi, i, 0)),
            pl.BlockSpec((1, FFN_HALO, d),
                         lambda bi, i: (bi, jnp.maximum(i * halo_blocks - 1, 0), 0)),
            pl.BlockSpec((1, tm, p.shape[-1]), lambda bi, i: (bi, i, 0)),
        ] + [_const_spec(a.shape) for a in consts],
        out_specs=pl.BlockSpec((1, tm, d), lambda bi, i: (bi, i, 0)),
        out_shape=jax.ShapeDtypeStruct((b, s, d), F32),
        compiler_params=pltpu.CompilerParams(
            dimension_semantics=("parallel", "parallel"), vmem_limit_bytes=VMEM_LIMIT_BYTES),
        name="ffn",
    )(h, h, p, *consts)


def _pad_cols(a, width):
    return jnp.pad(a, ((0, 0), (0, width - a.shape[1])))


def kernel(x, p, positions, norm1_g, w_in, q_norm_g, k_norm_g, lam_q1, lam_k1, lam_q2, lam_k2,
           subln_g, conv_w, conv_b, conv_ln_g, conv_ln_b, w_out, norm2_g, w_up, ffn_conv_w,
           ffn_conv_b, w_down, ple_norm_g, w_ple_gate, b_ple_gate, w_ple):
    b, s, d = x.shape
    depth = w_in.shape[0]
    assert s % (Q_CHUNKS * ROW_TILE) == 0 and d % LANES == 0
    d_ff = w_down.shape[1]
    d_ff_pad = -(-d_ff // LANES) * LANES
    nt = s // ROW_TILE

    pos_f = positions.astype(F32).reshape(b, nt, 1, ROW_TILE)
    invf = (ROPE_THETA ** (-jnp.arange(0, ROT_DIM, 2, dtype=F32) / ROT_DIM)).reshape(HALF_ROT, 1)
    row = lambda v: v.reshape(1, -1).astype(F32)
    col = lambda v: v.reshape(-1, 1).astype(F32)

    h = x
    for i in range(depth):
        lam_init = 0.8 - 0.6 * math.exp(-0.3 * i)
        qkv_cols = 2 * Q_COLS + ATTN_WIDTH
        wqkv_t = w_in[i][:, :qkv_cols].T.astype(BF16)
        wglu = w_in[i][:, qkv_cols:].astype(BF16)
        q_t, k, v_t, c = _in_proj(h, pos_f, row(norm1_g[i]), wqkv_t, wglu,
                                  col(q_norm_g[i]), col(k_norm_g[i]), invf)

        lamv = jnp.stack([lam_q1[i], lam_k1[i], lam_q2[i], lam_k2[i]]).astype(F32)
        bound = (1.01 * QK_DIM ** 0.5 * LOG2_E) * (jnp.max(jnp.abs(q_norm_g[i]))
                                                   * jnp.max(jnp.abs(k_norm_g[i])))
        attn_args = (q_t, k, v_t, lamv, col(subln_g[i]))
        attn = lax.cond(bound <= SCORE_BOUND,
                        lambda *a: _attn(*a, lam_init, True),
                        lambda *a: _attn(*a, lam_init, False), *attn_args)

        wo = w_out[i].astype(BF16)
        h = _mix(h, attn, c, conv_w[i].astype(F32), row(conv_b[i]), row(conv_ln_g[i]),
                 row(conv_ln_b[i]), wo[:ATTN_WIDTH], wo[ATTN_WIDTH:])

        wa = _pad_cols(w_up[i][:, :d_ff], d_ff_pad).astype(BF16)
        wb = _pad_cols(w_up[i][:, d_ff:], d_ff_pad).astype(BF16)
        fwa = _pad_cols(ffn_conv_w[i][:, :d_ff], d_ff_pad).astype(F32)
        fwb = _pad_cols(ffn_conv_w[i][:, d_ff:], d_ff_pad).astype(F32)
        fba = _pad_cols(row(ffn_conv_b[i][:d_ff]), d_ff_pad)
        fbb = _pad_cols(row(ffn_conv_b[i][d_ff:]), d_ff_pad)
        wd = jnp.pad(w_down[i], ((0, d_ff_pad - d_ff), (0, 0))).astype(BF16)
        h = _ffn(h, p[i], row(norm2_g[i]), wa, wb, fwa, fwb, fba, fbb, wd,
                 row(ple_norm_g[i]), w_ple_gate[i].astype(BF16), row(b_ple_gate[i]),
                 w_ple[i].astype(BF16))
    return h
```

```python
import functools
import math

import jax
import jax.numpy as jnp
from jax import lax
from jax.experimental import pallas as pl
from jax.experimental.pallas import tpu as pltpu

N_HEADS = 4
QK_DIM = 64
V_DIM = 2 * QK_DIM
ATTN_WIDTH = N_HEADS * V_DIM
Q_COLS = N_HEADS * 2 * QK_DIM
ROT_DIM = QK_DIM // 4
HALF_ROT = ROT_DIM // 2
ROPE_THETA = 500000.0
NORM_EPS = 1e-6
MASK_VALUE = -1e30
LOG2_E = math.log2(math.e)
V_AUG = V_DIM + 16

LANES = 128
SUBLANES = 8
VMEM_LIMIT_BYTES = 56 * 1024 * 1024

ROW_TILE = 512
IN_CHUNKS = 2
Q_CHUNKS = 2
STRIP = 512
SCORE_BOUND = 64.0
CONV_HALO = 32
FFN_HALO = 8

F32 = jnp.float32
BF16 = jnp.bfloat16


def _const_spec(shape):
    return pl.BlockSpec(shape, lambda *_: (0,) * len(shape), pipeline_mode=pl.Buffered(1))


def _rms_rows(x, g):
    ms = jnp.mean(x * x, axis=-1, keepdims=True)
    return x * lax.rsqrt(ms + NORM_EPS) * g


def _in_proj_kernel(h_ref, pos_ref, g1_ref, wqkv_t_ref, wglu_ref, qg_ref, kg_ref, invf_ref,
                    q_t_ref, k_ref, v_t_ref, c_ref, *, conv_width):
    hn = _rms_rows(h_ref[0], g1_ref[...]).astype(BF16)
    u_t = lax.dot_general(wqkv_t_ref[...], hn, (((1,), (1,)), ((), ())),
                          preferred_element_type=F32)
    glu = jnp.dot(hn, wglu_ref[...], preferred_element_type=F32)
    c_ref[0] = glu[:, :conv_width] * jax.nn.sigmoid(glu[:, conv_width:])

    ang = invf_ref[...] * pos_ref[0, 0]
    cos, sin = jnp.cos(ang), jnp.sin(ang)

    def norm_rope(x_t, g):
        segs = []
        for s in range(Q_COLS // QK_DIM):
            seg = x_t[s * QK_DIM:(s + 1) * QK_DIM]
            ms = jnp.mean(seg * seg, axis=0, keepdims=True)
            y = seg * lax.rsqrt(ms + NORM_EPS) * g
            x1, x2 = y[:HALF_ROT], y[HALF_ROT:ROT_DIM]
            segs += [x1 * cos - x2 * sin, x2 * cos + x1 * sin, y[ROT_DIM:]]
        return jnp.concatenate(segs, axis=0)

    q_t = norm_rope(u_t[:Q_COLS], qg_ref[...]) * (QK_DIM ** -0.5 * LOG2_E)
    k_t = norm_rope(u_t[Q_COLS:2 * Q_COLS], kg_ref[...])
    k_ref[0] = k_t.T.astype(BF16)
    t = q_t_ref.shape[-1]
    ones = jnp.ones((V_AUG - V_DIM, t), BF16)
    for ch in range(IN_CHUNKS):
        cols = slice(ch * t, (ch + 1) * t)
        q_t_ref[0, ch] = q_t[:, cols].astype(BF16)
        for hd in range(N_HEADS):
            v_rows = u_t[2 * Q_COLS + hd * V_DIM:2 * Q_COLS + (hd + 1) * V_DIM, cols]
            v_t_ref[0, ch, hd, :V_DIM] = v_rows.astype(BF16)
            v_t_ref[0, ch, hd, V_DIM:] = ones


def _in_proj(h, pos_f, g1, wqkv_t, wglu, qg, kg, invf):
    b, s, d = h.shape
    t = ROW_TILE
    tm = IN_CHUNKS * t
    nt = s // t
    conv_width = wglu.shape[1] // 2
    qkv_rows = wqkv_t.shape[0]
    return pl.pallas_call(
        functools.partial(_in_proj_kernel, conv_width=conv_width),
        grid=(b, s // tm),
        in_specs=[
            pl.BlockSpec((1, tm, d), lambda bi, i: (bi, i, 0)),
            pl.BlockSpec((1, 1, 1, tm), lambda bi, i: (bi, i, 0, 0)),
            _const_spec((1, d)),
            _const_spec((qkv_rows, d)),
            _const_spec((d, 2 * conv_width)),
            _const_spec((QK_DIM, 1)),
            _const_spec((QK_DIM, 1)),
            _const_spec((HALF_ROT, 1)),
        ],
        out_specs=[
            pl.BlockSpec((1, IN_CHUNKS, Q_COLS, t), lambda bi, i: (bi, i, 0, 0)),
            pl.BlockSpec((1, tm, Q_COLS), lambda bi, i: (bi, i, 0)),
            pl.BlockSpec((1, IN_CHUNKS, N_HEADS, V_AUG, t), lambda bi, i: (bi, i, 0, 0, 0)),
            pl.BlockSpec((1, tm, conv_width), lambda bi, i: (bi, i, 0)),
        ],
        out_shape=[
            jax.ShapeDtypeStruct((b, nt, Q_COLS, t), BF16),
            jax.ShapeDtypeStruct((b, s, Q_COLS), BF16),
            jax.ShapeDtypeStruct((b, nt, N_HEADS, V_AUG, t), BF16),
            jax.ShapeDtypeStruct((b, s, conv_width), F32),
        ],
        compiler_params=pltpu.CompilerParams(
            dimension_semantics=("parallel", "parallel"), vmem_limit_bytes=VMEM_LIMIT_BYTES),
        name="in_proj",
    )(h, pos_f, g1, wqkv_t, wglu, qg, kg, invf)


def _attn_kernel(q_t_ref, k_ref, v_t_ref, lamv_ref, subg_ref, o_ref, acc_ref, sa_ref, sb_ref,
                 pa_ref, pb_ref, *, lam_init):
    t = k_ref.shape[1] // v_t_ref.shape[1]
    tq = Q_CHUNKS * t
    qi = pl.program_id(2)
    q_t = jnp.concatenate([q_t_ref[0, c] for c in range(Q_CHUNKS)], axis=1)
    zero_half = jnp.zeros((QK_DIM, tq), BF16)
    q_maps = (jnp.concatenate([q_t[:QK_DIM], zero_half], axis=0),
              jnp.concatenate([zero_half, q_t[QK_DIM:]], axis=0))

    def keys(j):
        return k_ref[0, pl.ds(pl.multiple_of(j * t, t), t), :]

    units = [(idx, slice(c * STRIP, (c + 1) * STRIP)) for c in range(tq // STRIP)
             for idx in range(2)]

    def score_unit(kc, s_ref, idx, cols):
        s = jnp.dot(kc, q_maps[idx][:, cols], preferred_element_type=F32)
        s_ref[idx, :, cols] = s
        return jnp.max(s, axis=0, keepdims=True)

    def numerator_unit(s_ref, p_ref, idx, cols, m_new):
        for r in range(2):
            rows = slice(r * (t // 2), (r + 1) * (t // 2))
            p_ref[idx, rows, cols] = jnp.exp2((s_ref[idx, rows, cols] - m_new).astype(BF16))

    def accumulate_unit(vc, p_ref, idx, cols, alpha):
        acc_ref[idx, :, cols] = alpha * acc_ref[idx, :, cols] + jnp.dot(
            vc, p_ref[idx, :, cols], preferred_element_type=F32)

    def stage(j_acc, p_acc_ref, alphas, s_cur_ref, p_cur_ref, ms, cmax, j_next, s_next_ref):
        vc = v_t_ref[0, j_acc, 0]
        kc = keys(j_next)
        new_ms, new_alphas, new_cmax = [], [], []
        for u, (idx, cols) in enumerate(units):
            new_cmax.append(score_unit(kc, s_next_ref, idx, cols))
            m_new = jnp.maximum(ms[u], cmax[u])
            numerator_unit(s_cur_ref, p_cur_ref, idx, cols, m_new)
            new_ms.append(m_new)
            new_alphas.append(jnp.exp2(ms[u] - m_new))
            accumulate_unit(vc, p_acc_ref, idx, cols, alphas[u])
        return tuple(new_ms), tuple(new_alphas), tuple(new_cmax)

    d0 = Q_CHUNKS * qi
    tri = (lax.broadcasted_iota(jnp.int32, (t, t), 0)
           <= lax.broadcasted_iota(jnp.int32, (t, t), 1))
    acc_ref[...] = jnp.zeros_like(acc_ref)
    kc_first = keys(0)
    kc = keys(d0 + 1)
    ms, alphas, cmax = [], [], []
    m_lo = jnp.full((1, STRIP), MASK_VALUE, F32)
    one = jnp.ones((1, STRIP), F32)
    for idx, cols in units:
        cmax.append(score_unit(kc_first, sa_ref, idx, cols))
        if cols.stop <= t:
            pb_ref[idx, :, cols] = jnp.zeros((t, STRIP), BF16)
            ms.append(m_lo)
        else:
            s = jnp.dot(kc, q_maps[idx][:, cols], preferred_element_type=F32)
            s = jnp.where(tri[:, cols.start - t:cols.stop - t], s, MASK_VALUE)
            m = jnp.max(s, axis=0, keepdims=True)
            pb_ref[idx, :, cols] = jnp.exp2((s - m).astype(BF16))
            ms.append(m)
        alphas.append(one)
    carry = (tuple(ms), tuple(alphas), tuple(cmax))

    def body(jj, carry):
        ms, alphas, cmax = carry
        c0 = Q_CHUNKS * jj
        ms, alphas, cmax = stage(jnp.where(jj == 0, d0 + 1, c0 - 1), pb_ref, alphas,
                                 sa_ref, pa_ref, ms, cmax, c0 + 1, sb_ref)
        return stage(c0, pa_ref, alphas, sb_ref, pb_ref, ms, cmax, c0 + 2, sa_ref)

    ms, alphas, _ = lax.fori_loop(0, qi, body, carry)

    vc_pending = v_t_ref[0, jnp.where(qi == 0, d0 + 1, d0 - 1), 0]
    vc = v_t_ref[0, d0, 0]
    for u, (idx, cols) in enumerate(units):
        accumulate_unit(vc_pending, pb_ref, idx, cols, alphas[u])
        s = sa_ref[idx, :, cols]
        if cols.stop <= t:
            s = jnp.where(tri[:, cols], s, MASK_VALUE)
        m_new = jnp.maximum(ms[u], jnp.max(s, axis=0, keepdims=True))
        pa_ref[idx, :, cols] = jnp.exp2((s - m_new).astype(BF16))
        accumulate_unit(vc, pa_ref, idx, cols, jnp.exp2(ms[u] - m_new))

    o_ref[0] = _attn_finalize(acc_ref, lamv_ref, subg_ref, lam_init)


def _attn_bounded_kernel(q_t_ref, k_ref, v_t_ref, lamv_ref, subg_ref, o_ref, acc_ref, pa_ref,
                         pb_ref, l_ref, pc_ref, *, lam_init):
    t = k_ref.shape[1] // v_t_ref.shape[1]
    tq = Q_CHUNKS * t
    qi = pl.program_id(2)
    q_t = jnp.concatenate([q_t_ref[0, c] for c in range(Q_CHUNKS)], axis=1)
    zero_half = jnp.zeros((QK_DIM, tq), BF16)
    q_maps = (jnp.concatenate([q_t[:QK_DIM], zero_half], axis=0),
              jnp.concatenate([zero_half, q_t[QK_DIM:]], axis=0))

    def keys(j):
        return k_ref[0, pl.ds(pl.multiple_of(j * t, t), t), :]

    units = [(idx, slice(c * STRIP, (c + 1) * STRIP)) for c in range(tq // STRIP)
             for idx in range(2)]
    tri = (lax.broadcasted_iota(jnp.int32, (t, t), 0)
           <= lax.broadcasted_iota(jnp.int32, (t, t), 1))

    def numerator_unit(kc, p_ref, idx, cols, mask=None, l_cols=None):
        l_cols = cols if l_cols is None else l_cols
        s = jnp.dot(kc, q_maps[idx][:, l_cols], preferred_element_type=F32)
        if mask is not None:
            s = jnp.where(mask, s, MASK_VALUE)
        p = jnp.exp2(s)
        p_ref[idx, :, cols] = p.astype(BF16)
        l_ref[idx, :, l_cols] += jnp.sum(p.reshape(t // SUBLANES, SUBLANES, STRIP), axis=0)

    def accumulate_unit(vc, p_ref, idx, cols):
        acc_ref[idx, :V_DIM, cols] += jnp.dot(vc[:V_DIM], p_ref[idx, :, cols],
                                              preferred_element_type=F32)

    def stage(j_acc, p_acc_ref, j_new, p_new_ref):
        vc = v_t_ref[0, j_acc, 0]
        kc = keys(j_new)
        for idx, cols in units:
            numerator_unit(kc, p_new_ref, idx, cols)
            accumulate_unit(vc, p_acc_ref, idx, cols)

    d0 = Q_CHUNKS * qi
    acc_ref[...] = jnp.zeros_like(acc_ref)
    l_ref[...] = jnp.zeros_like(l_ref)
    kc_hi, kc_lo = keys(d0 + 1), keys(d0)
    vc_hi = v_t_ref[0, d0 + 1, 0]
    for idx, cols in units:
        if cols.stop <= t:
            numerator_unit(kc_lo, pa_ref, idx, cols, tri[:, cols])
        else:
            numerator_unit(kc_lo, pa_ref, idx, cols)
            numerator_unit(kc_hi, pc_ref, idx, slice(cols.start - t, cols.stop - t),
                           tri[:, cols.start - t:cols.stop - t], l_cols=cols)

    def chunk_pair(jj):
        c0 = Q_CHUNKS * jj
        stage(jnp.where(jj == 0, d0, c0 - 1), pa_ref, c0, pb_ref)
        stage(c0, pb_ref, c0 + 1, pa_ref)

    def body(jj, carry):
        chunk_pair(2 * jj)
        chunk_pair(2 * jj + 1)
        return carry

    lax.fori_loop(0, qi // 2, body, 0)

    @pl.when(qi % 2 == 1)
    def _():
        chunk_pair(qi - 1)

    vc = v_t_ref[0, jnp.where(qi == 0, d0, d0 - 1), 0]
    for idx, cols in units:
        accumulate_unit(vc, pa_ref, idx, cols)
        if cols.start >= t:
            acc_ref[idx, :V_DIM, cols] += jnp.dot(
                vc_hi[:V_DIM], pc_ref[idx, :, cols.start - t:cols.stop - t],
                preferred_element_type=F32)

    for idx in range(2):
        acc_ref[idx, V_DIM:V_DIM + 1] = jnp.sum(l_ref[idx], axis=0, keepdims=True)
    o_ref[0] = _attn_finalize(acc_ref, lamv_ref, subg_ref, lam_init)


def _attn_finalize(acc_ref, lamv_ref, subg_ref, lam_init):
    lamv = lamv_ref[...]
    lam = (jnp.exp(jnp.sum(lamv[0:1] * lamv[1:2], axis=-1, keepdims=True))
           - jnp.exp(jnp.sum(lamv[2:3] * lamv[3:4], axis=-1, keepdims=True)) + lam_init)
    o1 = acc_ref[0, :V_DIM] * (1.0 / acc_ref[0, V_DIM:V_DIM + 1])
    o2 = acc_ref[1, :V_DIM] * (1.0 / acc_ref[1, V_DIM:V_DIM + 1])
    o = o1 - lam * o2
    ms_o = jnp.mean(o * o, axis=0, keepdims=True)
    y = o * lax.rsqrt(ms_o + NORM_EPS) * subg_ref[...] * (1.0 - lam_init)
    return y.T.astype(BF16)


def _attn(q_t, k, v_t, lamv, subg, lam_init, bounded):
    b, nt, _, t = q_t.shape
    s = k.shape[1]
    tq = Q_CHUNKS * t
    p_bufs = [pltpu.VMEM((2, t, tq), BF16), pltpu.VMEM((2, t, tq), BF16)]
    s_bufs = [] if bounded else [pltpu.VMEM((2, t, tq), F32), pltpu.VMEM((2, t, tq), F32)]
    body = _attn_bounded_kernel if bounded else _attn_kernel
    return pl.pallas_call(
        functools.partial(body, lam_init=lam_init),
        grid=(b, N_HEADS, s // tq),
        in_specs=[
            pl.BlockSpec((1, Q_CHUNKS, 2 * QK_DIM, t), lambda bi, hi, qi: (bi, qi, hi, 0)),
            pl.BlockSpec((1, s, 2 * QK_DIM), lambda bi, hi, qi: (bi, 0, hi)),
            pl.BlockSpec((1, nt, 1, V_AUG, t), lambda bi, hi, qi: (bi, 0, hi, 0, 0)),
            _const_spec((4, QK_DIM)),
            _const_spec((V_DIM, 1)),
        ],
        out_specs=pl.BlockSpec((1, tq, V_DIM), lambda bi, hi, qi: (bi, qi, hi)),
        out_shape=jax.ShapeDtypeStruct((b, s, ATTN_WIDTH), BF16),
        scratch_shapes=([pltpu.VMEM((2, V_AUG, tq), F32)] + s_bufs + p_bufs
                        + ([pltpu.VMEM((2, SUBLANES, tq), F32),
                            pltpu.VMEM((2, t, tq - t), BF16)] if bounded else [])),
        compiler_params=pltpu.CompilerParams(
            dimension_semantics=("parallel", "parallel", "arbitrary"),
            vmem_limit_bytes=VMEM_LIMIT_BYTES),
        name="attn_bounded" if bounded else "attn",
    )(q_t, k, v_t, lamv, subg)


def _mix_kernel(h_ref, attn_ref, c_ref, halo_ref, cw_ref, cb_ref, lng_ref, lnb_ref,
                wo_attn_ref, wo_conv_ref, o_ref, buf_ref, shift_ref):
    tm = c_ref.shape[1]
    taps = cw_ref.shape[0]
    first = pl.program_id(1) == 0
    halo = halo_ref[0]
    buf_ref[:CONV_HALO] = jnp.where(first, jnp.zeros_like(halo), halo)
    buf_ref[CONV_HALO:] = c_ref[0]
    base = CONV_HALO - (taps - 1)
    span = shift_ref.shape[1]
    for r in range(1, SUBLANES):
        shift_ref[r] = buf_ref[pl.ds(r, span), :]
    cw = cw_ref[...]
    y = cb_ref[...]
    for j in range(taps):
        r, a = (base + j) % SUBLANES, (base + j) // SUBLANES
        src = buf_ref if r == 0 else shift_ref.at[r]
        y = y + cw[j:j + 1] * src[pl.ds(a * SUBLANES, tm), :]
    mu = jnp.mean(y, axis=-1, keepdims=True)
    yc = y - mu
    z = yc * lax.rsqrt(jnp.mean(yc * yc, axis=-1, keepdims=True) + NORM_EPS)
    z = z * lng_ref[...] + lnb_ref[...]
    z = (z * jax.nn.sigmoid(z)).astype(BF16)
    upd = (jnp.dot(attn_ref[0], wo_attn_ref[...], preferred_element_type=F32)
           + jnp.dot(z, wo_conv_ref[...], preferred_element_type=F32))
    o_ref[0] = h_ref[0] + upd


def _mix(h, attn, c, cw, cb, lng, lnb, wo_attn, wo_conv):
    b, s, d = h.shape
    tm = ROW_TILE
    cwid = c.shape[-1]
    halo_blocks = tm // CONV_HALO
    return pl.pallas_call(
        _mix_kernel,
        grid=(b, s // tm),
        in_specs=[
            pl.BlockSpec((1, tm, d), lambda bi, i: (bi, i, 0)),
            pl.BlockSpec((1, tm, ATTN_WIDTH), lambda bi, i: (bi, i, 0)),
            pl.BlockSpec((1, tm, cwid), lambda bi, i: (bi, i, 0)),
            pl.BlockSpec((1, CONV_HALO, cwid),
                         lambda bi, i: (bi, jnp.maximum(i * halo_blocks - 1, 0), 0)),
            _const_spec(cw.shape),
            _const_spec((1, cwid)),
            _const_spec((1, cwid)),
            _const_spec((1, cwid)),
            _const_spec(wo_attn.shape),
            _const_spec(wo_conv.shape),
        ],
        out_specs=pl.BlockSpec((1, tm, d), lambda bi, i: (bi, i, 0)),
        out_shape=jax.ShapeDtypeStruct((b, s, d), F32),
        scratch_shapes=[pltpu.VMEM((CONV_HALO + tm, cwid), F32),
                        pltpu.VMEM((SUBLANES, CONV_HALO + tm - SUBLANES, cwid), F32)],
        compiler_params=pltpu.CompilerParams(
            dimension_semantics=("parallel", "parallel"), vmem_limit_bytes=VMEM_LIMIT_BYTES),
        name="mix",
    )(h, attn, c, c, cw, cb, lng, lnb, wo_attn, wo_conv)


def _ffn_kernel(h_ref, halo_ref, p_ref, g2_ref, wa_ref, wb_ref, fwa_ref, fwb_ref, fba_ref,
                fbb_ref, wd_ref, pg_ref, wg_ref, bg_ref, wp_ref, o_ref):
    tm = h_ref.shape[1]
    first = pl.program_id(1) == 0
    h = h_ref[0]
    halo = halo_ref[0]
    halo = jnp.where(first, jnp.zeros_like(halo), halo)
    x = jnp.concatenate([halo, h], axis=0)
    xn = _rms_rows(x, g2_ref[...]).astype(BF16)

    def conv_up(w_ref, cw_ref, cb_ref):
        u = jnp.dot(xn, w_ref[...], preferred_element_type=F32)
        cw = cw_ref[...]
        y = cb_ref[...] + cw[2:3] * u[FFN_HALO:]
        y = y + cw[1:2] * u[FFN_HALO - 1:FFN_HALO - 1 + tm]
        y = y + cw[0:1] * u[FFN_HALO - 2:FFN_HALO - 2 + tm]
        return y

    ya = conv_up(wa_ref, fwa_ref, fba_ref)
    yb = conv_up(wb_ref, fwb_ref, fbb_ref)
    act = (ya * jax.nn.sigmoid(ya) * yb).astype(BF16)
    h = h + jnp.dot(act, wd_ref[...], preferred_element_type=F32)

    hn = _rms_rows(h, pg_ref[...]).astype(BF16)
    gate = jax.nn.sigmoid(jnp.dot(hn, wg_ref[...], preferred_element_type=F32) + bg_ref[...])
    ple = jnp.dot(p_ref[0].astype(BF16), wp_ref[...], preferred_element_type=F32)
    o_ref[0] = h + gate * ple


def _ffn(h, p, g2, wa, wb, fwa, fwb, fba, fbb, wd, pg, wg, bg, wp):
    b, s, d = h.shape
    tm = ROW_TILE
    halo_blocks = tm // FFN_HALO
    consts = (g2, wa, wb, fwa, fwb, fba, fbb, wd, pg, wg, bg, wp)
    return pl.pallas_call(
        _ffn_kernel,
        grid=(b, s // tm),
        in_specs=[
            pl.BlockSpec((1, tm, d), lambda bi, i: (bi, i, 0)),
            pl.BlockSpec((1, FFN_HALO, d),
                         lambda bi, i: (bi, jnp.maximum(i * halo_blocks - 1, 0), 0)),
            pl.BlockSpec((1, tm, p.shape[-1]), lambda bi, i: (bi, i, 0)),
        ] + [_const_spec(a.shape) for a in consts],
        out_specs=pl.BlockSpec((1, tm, d), lambda bi, i: (bi, i, 0)),
        out_shape=jax.ShapeDtypeStruct((b, s, d), F32),
        compiler_params=pltpu.CompilerParams(
            dimension_semantics=("parallel", "parallel"), vmem_limit_bytes=VMEM_LIMIT_BYTES),
        name="ffn",
    )(h, h, p, *consts)


def _pad_cols(a, width):
    return jnp.pad(a, ((0, 0), (0, width - a.shape[1])))


def kernel(x, p, positions, norm1_g, w_in, q_norm_g, k_norm_g, lam_q1, lam_k1, lam_q2, lam_k2,
           subln_g, conv_w, conv_b, conv_ln_g, conv_ln_b, w_out, norm2_g, w_up, ffn_conv_w,
           ffn_conv_b, w_down, ple_norm_g, w_ple_gate, b_ple_gate, w_ple):
    b, s, d = x.shape
    depth = w_in.shape[0]
    assert s % (Q_CHUNKS * ROW_TILE) == 0 and s % (IN_CHUNKS * ROW_TILE) == 0 and d % LANES == 0
    d_ff = w_down.shape[1]
    d_ff_pad = -(-d_ff // LANES) * LANES
    nt = s // ROW_TILE

    pos_f = positions.astype(F32).reshape(b, nt // IN_CHUNKS, 1, IN_CHUNKS * ROW_TILE)
    invf = (ROPE_THETA ** (-jnp.arange(0, ROT_DIM, 2, dtype=F32) / ROT_DIM)).reshape(HALF_ROT, 1)
    row = lambda v: v.reshape(1, -1).astype(F32)
    col = lambda v: v.reshape(-1, 1).astype(F32)

    h = x
    for i in range(depth):
        lam_init = 0.8 - 0.6 * math.exp(-0.3 * i)
        qkv_cols = 2 * Q_COLS + ATTN_WIDTH
        wqkv_t = w_in[i][:, :qkv_cols].T.astype(BF16)
        wglu = w_in[i][:, qkv_cols:].astype(BF16)
        q_t, k, v_t, c = _in_proj(h, pos_f, row(norm1_g[i]), wqkv_t, wglu,
                                  col(q_norm_g[i]), col(k_norm_g[i]), invf)

        lamv = jnp.stack([lam_q1[i], lam_k1[i], lam_q2[i], lam_k2[i]]).astype(F32)
        bound = (1.01 * QK_DIM ** 0.5 * LOG2_E) * (jnp.max(jnp.abs(q_norm_g[i]))
                                                   * jnp.max(jnp.abs(k_norm_g[i])))
        attn_args = (q_t, k, v_t, lamv, col(subln_g[i]))
        attn = lax.cond(bound <= SCORE_BOUND,
                        lambda *a: _attn(*a, lam_init, True),
                        lambda *a: _attn(*a, lam_init, False), *attn_args)

        wo = w_out[i].astype(BF16)
        h = _mix(h, attn, c, conv_w[i].astype(F32), row(conv_b[i]), row(conv_ln_g[i]),
                 row(conv_ln_b[i]), wo[:ATTN_WIDTH], wo[ATTN_WIDTH:])

        wa = _pad_cols(w_up[i][:, :d_ff], d_ff_pad).astype(BF16)
        wb = _pad_cols(w_up[i][:, d_ff:], d_ff_pad).astype(BF16)
        fwa = _pad_cols(ffn_conv_w[i][:, :d_ff], d_ff_pad).astype(F32)
        fwb = _pad_cols(ffn_conv_w[i][:, d_ff:], d_ff_pad).astype(F32)
        fba = _pad_cols(row(ffn_conv_b[i][:d_ff]), d_ff_pad)
        fbb = _pad_cols(row(ffn_conv_b[i][d_ff:]), d_ff_pad)
        wd = jnp.pad(w_down[i], ((0, d_ff_pad - d_ff), (0, 0))).astype(BF16)
        h = _ffn(h, p[i], row(norm2_g[i]), wa, wb, fwa, fwb, fba, fbb, wd,
                 row(ple_norm_g[i]), w_ple_gate[i].astype(BF16), row(b_ple_gate[i]),
                 w_ple[i].astype(BF16))
    return h
```

```python
import functools
import math

import jax
import jax.numpy as jnp
from jax import lax
from jax.experimental import pallas as pl
from jax.experimental.pallas import tpu as pltpu

N_HEADS = 4
QK_DIM = 64
V_DIM = 2 * QK_DIM
ATTN_WIDTH = N_HEADS * V_DIM
Q_COLS = N_HEADS * 2 * QK_DIM
ROT_DIM = QK_DIM // 4
HALF_ROT = ROT_DIM // 2
ROPE_THETA = 500000.0
NORM_EPS = 1e-6
MASK_VALUE = -1e30
LOG2_E = math.log2(math.e)
V_AUG = V_DIM + 16

LANES = 128
SUBLANES = 8
VMEM_LIMIT_BYTES = 56 * 1024 * 1024

ROW_TILE = 512
IN_CHUNKS = 2
Q_CHUNKS = 2
STRIP = 512
SCORE_BOUND = 64.0
CONV_HALO = 32
FFN_HALO = 8

F32 = jnp.float32
BF16 = jnp.bfloat16


def _const_spec(shape):
    return pl.BlockSpec(shape, lambda *_: (0,) * len(shape), pipeline_mode=pl.Buffered(1))


def _rms_rows(x, g):
    ms = jnp.mean(x * x, axis=-1, keepdims=True)
    return x * lax.rsqrt(ms + NORM_EPS) * g


def _transpose_cast_kernel(w_ref, o_ref):
    o_ref[...] = w_ref[0].T.astype(BF16)


def _qkv_weight_t(w_in, layer, cols):
    _, d, _ = w_in.shape
    bn = ROW_TILE
    return pl.pallas_call(
        _transpose_cast_kernel,
        grid=(cols // bn,),
        in_specs=[pl.BlockSpec((1, d, bn), lambda j: (layer, 0, j))],
        out_specs=pl.BlockSpec((bn, d), lambda j: (j, 0)),
        out_shape=jax.ShapeDtypeStruct((cols, d), BF16),
        compiler_params=pltpu.CompilerParams(
            dimension_semantics=("parallel",), vmem_limit_bytes=VMEM_LIMIT_BYTES),
        name="qkv_weight_t",
    )(w_in)


def _in_proj_kernel(h_ref, pos_ref, g1_ref, wqkv_t_ref, wglu_ref, qg_ref, kg_ref, invf_ref,
                    q_t_ref, k_ref, v_t_ref, c_ref, *, conv_width):
    hn = _rms_rows(h_ref[0], g1_ref[...]).astype(BF16)
    u_t = lax.dot_general(wqkv_t_ref[...], hn, (((1,), (1,)), ((), ())),
                          preferred_element_type=F32)
    glu = jnp.dot(hn, wglu_ref[...], preferred_element_type=F32)
    c_ref[0] = glu[:, :conv_width] * jax.nn.sigmoid(glu[:, conv_width:])

    ang = invf_ref[...] * pos_ref[0, 0]
    cos, sin = jnp.cos(ang), jnp.sin(ang)

    def norm_rope(x_t, g):
        segs = []
        for s in range(Q_COLS // QK_DIM):
            seg = x_t[s * QK_DIM:(s + 1) * QK_DIM]
            ms = jnp.mean(seg * seg, axis=0, keepdims=True)
            y = seg * lax.rsqrt(ms + NORM_EPS) * g
            x1, x2 = y[:HALF_ROT], y[HALF_ROT:ROT_DIM]
            segs += [x1 * cos - x2 * sin, x2 * cos + x1 * sin, y[ROT_DIM:]]
        return jnp.concatenate(segs, axis=0)

    q_t = norm_rope(u_t[:Q_COLS], qg_ref[...]) * (QK_DIM ** -0.5 * LOG2_E)
    k_t = norm_rope(u_t[Q_COLS:2 * Q_COLS], kg_ref[...])
    k_ref[0] = k_t.T.astype(BF16)
    t = q_t_ref.shape[-1]
    ones = jnp.ones((V_AUG - V_DIM, t), BF16)
    for ch in range(IN_CHUNKS):
        cols = slice(ch * t, (ch + 1) * t)
        q_t_ref[0, ch] = q_t[:, cols].astype(BF16)
        for hd in range(N_HEADS):
            v_rows = u_t[2 * Q_COLS + hd * V_DIM:2 * Q_COLS + (hd + 1) * V_DIM, cols]
            v_t_ref[0, ch, hd, :V_DIM] = v_rows.astype(BF16)
            v_t_ref[0, ch, hd, V_DIM:] = ones


def _in_proj(h, pos_f, g1, wqkv_t, wglu, qg, kg, invf):
    b, s, d = h.shape
    t = ROW_TILE
    tm = IN_CHUNKS * t
    nt = s // t
    conv_width = wglu.shape[1] // 2
    qkv_rows = wqkv_t.shape[0]
    return pl.pallas_call(
        functools.partial(_in_proj_kernel, conv_width=conv_width),
        grid=(b, s // tm),
        in_specs=[
            pl.BlockSpec((1, tm, d), lambda bi, i: (bi, i, 0)),
            pl.BlockSpec((1, 1, 1, tm), lambda bi, i: (bi, i, 0, 0)),
            _const_spec((1, d)),
            _const_spec((qkv_rows, d)),
            _const_spec((d, 2 * conv_width)),
            _const_spec((QK_DIM, 1)),
            _const_spec((QK_DIM, 1)),
            _const_spec((HALF_ROT, 1)),
        ],
        out_specs=[
            pl.BlockSpec((1, IN_CHUNKS, Q_COLS, t), lambda bi, i: (bi, i, 0, 0)),
            pl.BlockSpec((1, tm, Q_COLS), lambda bi, i: (bi, i, 0)),
            pl.BlockSpec((1, IN_CHUNKS, N_HEADS, V_AUG, t), lambda bi, i: (bi, i, 0, 0, 0)),
            pl.BlockSpec((1, tm, conv_width), lambda bi, i: (bi, i, 0)),
        ],
        out_shape=[
            jax.ShapeDtypeStruct((b, nt, Q_COLS, t), BF16),
            jax.ShapeDtypeStruct((b, s, Q_COLS), BF16),
            jax.ShapeDtypeStruct((b, nt, N_HEADS, V_AUG, t), BF16),
            jax.ShapeDtypeStruct((b, s, conv_width), F32),
        ],
        compiler_params=pltpu.CompilerParams(
            dimension_semantics=("parallel", "parallel"), vmem_limit_bytes=VMEM_LIMIT_BYTES),
        name="in_proj",
    )(h, pos_f, g1, wqkv_t, wglu, qg, kg, invf)


def _attn_kernel(q_t_ref, k_ref, v_t_ref, lamv_ref, subg_ref, o_ref, acc_ref, sa_ref, sb_ref,
                 pa_ref, pb_ref, *, lam_init):
    t = k_ref.shape[1] // v_t_ref.shape[1]
    tq = Q_CHUNKS * t
    qi = pl.program_id(2)
    q_t = jnp.concatenate([q_t_ref[0, c] for c in range(Q_CHUNKS)], axis=1)
    zero_half = jnp.zeros((QK_DIM, tq), BF16)
    q_maps = (jnp.concatenate([q_t[:QK_DIM], zero_half], axis=0),
              jnp.concatenate([zero_half, q_t[QK_DIM:]], axis=0))

    def keys(j):
        return k_ref[0, pl.ds(pl.multiple_of(j * t, t), t), :]

    units = [(idx, slice(c * STRIP, (c + 1) * STRIP)) for c in range(tq // STRIP)
             for idx in range(2)]

    def score_unit(kc, s_ref, idx, cols):
        s = jnp.dot(kc, q_maps[idx][:, cols], preferred_element_type=F32)
        s_ref[idx, :, cols] = s
        return jnp.max(s, axis=0, keepdims=True)

    def numerator_unit(s_ref, p_ref, idx, cols, m_new):
        for r in range(2):
            rows = slice(r * (t // 2), (r + 1) * (t // 2))
            p_ref[idx, rows, cols] = jnp.exp2((s_ref[idx, rows, cols] - m_new).astype(BF16))

    def accumulate_unit(vc, p_ref, idx, cols, alpha):
        acc_ref[idx, :, cols] = alpha * acc_ref[idx, :, cols] + jnp.dot(
            vc, p_ref[idx, :, cols], preferred_element_type=F32)

    def stage(j_acc, p_acc_ref, alphas, s_cur_ref, p_cur_ref, ms, cmax, j_next, s_next_ref):
        vc = v_t_ref[0, j_acc, 0]
        kc = keys(j_next)
        new_ms, new_alphas, new_cmax = [], [], []
        for u, (idx, cols) in enumerate(units):
            new_cmax.append(score_unit(kc, s_next_ref, idx, cols))
            m_new = jnp.maximum(ms[u], cmax[u])
            numerator_unit(s_cur_ref, p_cur_ref, idx, cols, m_new)
            new_ms.append(m_new)
            new_alphas.append(jnp.exp2(ms[u] - m_new))
            accumulate_unit(vc, p_acc_ref, idx, cols, alphas[u])
        return tuple(new_ms), tuple(new_alphas), tuple(new_cmax)

    d0 = Q_CHUNKS * qi
    tri = (lax.broadcasted_iota(jnp.int32, (t, t), 0)
           <= lax.broadcasted_iota(jnp.int32, (t, t), 1))
    acc_ref[...] = jnp.zeros_like(acc_ref)
    kc_first = keys(0)
    kc = keys(d0 + 1)
    ms, alphas, cmax = [], [], []
    m_lo = jnp.full((1, STRIP), MASK_VALUE, F32)
    one = jnp.ones((1, STRIP), F32)
    for idx, cols in units:
        cmax.append(score_unit(kc_first, sa_ref, idx, cols))
        if cols.stop <= t:
            pb_ref[idx, :, cols] = jnp.zeros((t, STRIP), BF16)
            ms.append(m_lo)
        else:
            s = jnp.dot(kc, q_maps[idx][:, cols], preferred_element_type=F32)
            s = jnp.where(tri[:, cols.start - t:cols.stop - t], s, MASK_VALUE)
            m = jnp.max(s, axis=0, keepdims=True)
            pb_ref[idx, :, cols] = jnp.exp2((s - m).astype(BF16))
            ms.append(m)
        alphas.append(one)
    carry = (tuple(ms), tuple(alphas), tuple(cmax))

    def body(jj, carry):
        ms, alphas, cmax = carry
        c0 = Q_CHUNKS * jj
        ms, alphas, cmax = stage(jnp.where(jj == 0, d0 + 1, c0 - 1), pb_ref, alphas,
                                 sa_ref, pa_ref, ms, cmax, c0 + 1, sb_ref)
        return stage(c0, pa_ref, alphas, sb_ref, pb_ref, ms, cmax, c0 + 2, sa_ref)

    ms, alphas, _ = lax.fori_loop(0, qi, body, carry)

    vc_pending = v_t_ref[0, jnp.where(qi == 0, d0 + 1, d0 - 1), 0]
    vc = v_t_ref[0, d0, 0]
    for u, (idx, cols) in enumerate(units):
        accumulate_unit(vc_pending, pb_ref, idx, cols, alphas[u])
        s = sa_ref[idx, :, cols]
        if cols.stop <= t:
            s = jnp.where(tri[:, cols], s, MASK_VALUE)
        m_new = jnp.maximum(ms[u], jnp.max(s, axis=0, keepdims=True))
        pa_ref[idx, :, cols] = jnp.exp2((s - m_new).astype(BF16))
        accumulate_unit(vc, pa_ref, idx, cols, jnp.exp2(ms[u] - m_new))

    o_ref[0] = _attn_finalize(acc_ref, lamv_ref, subg_ref, lam_init)


def _attn_bounded_kernel(q_t_ref, k_ref, v_t_ref, lamv_ref, subg_ref, o_ref, acc_ref, pa_ref,
                         pb_ref, l_ref, pc_ref, *, lam_init):
    t = k_ref.shape[1] // v_t_ref.shape[1]
    tq = Q_CHUNKS * t
    qi = pl.program_id(2)
    q_t = jnp.concatenate([q_t_ref[0, c] for c in range(Q_CHUNKS)], axis=1)
    zero_half = jnp.zeros((QK_DIM, tq), BF16)
    q_maps = (jnp.concatenate([q_t[:QK_DIM], zero_half], axis=0),
              jnp.concatenate([zero_half, q_t[QK_DIM:]], axis=0))

    def keys(j):
        return k_ref[0, pl.ds(pl.multiple_of(j * t, t), t), :]

    units = [(idx, slice(c * STRIP, (c + 1) * STRIP)) for c in range(tq // STRIP)
             for idx in range(2)]
    tri = (lax.broadcasted_iota(jnp.int32, (t, t), 0)
           <= lax.broadcasted_iota(jnp.int32, (t, t), 1))

    def numerator_unit(kc, p_ref, idx, cols, mask=None, l_cols=None):
        l_cols = cols if l_cols is None else l_cols
        s = jnp.dot(kc, q_maps[idx][:, l_cols], preferred_element_type=F32)
        if mask is not None:
            s = jnp.where(mask, s, MASK_VALUE)
        p = jnp.exp2(s)
        p_ref[idx, :, cols] = p.astype(BF16)
        l_ref[idx, :, l_cols] += jnp.sum(p.reshape(t // SUBLANES, SUBLANES, STRIP), axis=0)

    def accumulate_unit(vc, p_ref, idx, cols):
        acc_ref[idx, :V_DIM, cols] += jnp.dot(vc[:V_DIM], p_ref[idx, :, cols],
                                              preferred_element_type=F32)

    def stage(j_acc, p_acc_ref, j_new, p_new_ref):
        vc = v_t_ref[0, j_acc, 0]
        kc = keys(j_new)
        for idx, cols in units:
            numerator_unit(kc, p_new_ref, idx, cols)
            accumulate_unit(vc, p_acc_ref, idx, cols)

    d0 = Q_CHUNKS * qi
    acc_ref[...] = jnp.zeros_like(acc_ref)
    l_ref[...] = jnp.zeros_like(l_ref)
    kc_hi, kc_lo = keys(d0 + 1), keys(d0)
    vc_hi = v_t_ref[0, d0 + 1, 0]
    for idx, cols in units:
        if cols.stop <= t:
            numerator_unit(kc_lo, pa_ref, idx, cols, tri[:, cols])
        else:
            numerator_unit(kc_lo, pa_ref, idx, cols)
            numerator_unit(kc_hi, pc_ref, idx, slice(cols.start - t, cols.stop - t),
                           tri[:, cols.start - t:cols.stop - t], l_cols=cols)

    def chunk_pair(jj):
        c0 = Q_CHUNKS * jj
        stage(jnp.where(jj == 0, d0, c0 - 1), pa_ref, c0, pb_ref)
        stage(c0, pb_ref, c0 + 1, pa_ref)

    def body(jj, carry):
        chunk_pair(2 * jj)
        chunk_pair(2 * jj + 1)
        return carry

    lax.fori_loop(0, qi // 2, body, 0)

    @pl.when(qi % 2 == 1)
    def _():
        chunk_pair(qi - 1)

    vc = v_t_ref[0, jnp.where(qi == 0, d0, d0 - 1), 0]
    for idx, cols in units:
        accumulate_unit(vc, pa_ref, idx, cols)
        if cols.start >= t:
            acc_ref[idx, :V_DIM, cols] += jnp.dot(
                vc_hi[:V_DIM], pc_ref[idx, :, cols.start - t:cols.stop - t],
                preferred_element_type=F32)

    for idx in range(2):
        acc_ref[idx, V_DIM:V_DIM + 1] = jnp.sum(l_ref[idx], axis=0, keepdims=True)
    o_ref[0] = _attn_finalize(acc_ref, lamv_ref, subg_ref, lam_init)


def _attn_finalize(acc_ref, lamv_ref, subg_ref, lam_init):
    lamv = lamv_ref[...]
    lam = (jnp.exp(jnp.sum(lamv[0:1] * lamv[1:2], axis=-1, keepdims=True))
           - jnp.exp(jnp.sum(lamv[2:3] * lamv[3:4], axis=-1, keepdims=True)) + lam_init)
    o1 = acc_ref[0, :V_DIM] * (1.0 / acc_ref[0, V_DIM:V_DIM + 1])
    o2 = acc_ref[1, :V_DIM] * (1.0 / acc_ref[1, V_DIM:V_DIM + 1])
    o = o1 - lam * o2
    ms_o = jnp.mean(o * o, axis=0, keepdims=True)
    y = o * lax.rsqrt(ms_o + NORM_EPS) * subg_ref[...] * (1.0 - lam_init)
    return y.T.astype(BF16)


def _attn(q_t, k, v_t, lamv, subg, lam_init, bounded):
    b, nt, _, t = q_t.shape
    s = k.shape[1]
    tq = Q_CHUNKS * t
    p_bufs = [pltpu.VMEM((2, t, tq), BF16), pltpu.VMEM((2, t, tq), BF16)]
    s_bufs = [] if bounded else [pltpu.VMEM((2, t, tq), F32), pltpu.VMEM((2, t, tq), F32)]
    body = _attn_bounded_kernel if bounded else _attn_kernel
    return pl.pallas_call(
        functools.partial(body, lam_init=lam_init),
        grid=(b, N_HEADS, s // tq),
        in_specs=[
            pl.BlockSpec((1, Q_CHUNKS, 2 * QK_DIM, t), lambda bi, hi, qi: (bi, qi, hi, 0)),
            pl.BlockSpec((1, s, 2 * QK_DIM), lambda bi, hi, qi: (bi, 0, hi)),
            pl.BlockSpec((1, nt, 1, V_AUG, t), lambda bi, hi, qi: (bi, 0, hi, 0, 0)),
            _const_spec((4, QK_DIM)),
            _const_spec((V_DIM, 1)),
        ],
        out_specs=pl.BlockSpec((1, tq, V_DIM), lambda bi, hi, qi: (bi, qi, hi)),
        out_shape=jax.ShapeDtypeStruct((b, s, ATTN_WIDTH), BF16),
        scratch_shapes=([pltpu.VMEM((2, V_AUG, tq), F32)] + s_bufs + p_bufs
                        + ([pltpu.VMEM((2, SUBLANES, tq), F32),
                            pltpu.VMEM((2, t, tq - t), BF16)] if bounded else [])),
        compiler_params=pltpu.CompilerParams(
            dimension_semantics=("parallel", "parallel", "arbitrary"),
            vmem_limit_bytes=VMEM_LIMIT_BYTES),
        name="attn_bounded" if bounded else "attn",
    )(q_t, k, v_t, lamv, subg)


def _mix_kernel(h_ref, attn_ref, c_ref, halo_ref, cw_ref, cb_ref, lng_ref, lnb_ref,
                wo_ref, o_ref, buf_ref, shift_ref):
    tm = c_ref.shape[1]
    taps = cw_ref.shape[0]
    first = pl.program_id(1) == 0
    halo = halo_ref[0]
    buf_ref[:CONV_HALO] = jnp.where(first, jnp.zeros_like(halo), halo)
    buf_ref[CONV_HALO:] = c_ref[0]
    base = CONV_HALO - (taps - 1)
    span = shift_ref.shape[1]
    for r in range(1, SUBLANES):
        shift_ref[r] = buf_ref[pl.ds(r, span), :]
    cw = cw_ref[...]
    y = cb_ref[...]
    for j in range(taps):
        r, a = (base + j) % SUBLANES, (base + j) // SUBLANES
        src = buf_ref if r == 0 else shift_ref.at[r]
        y = y + cw[j:j + 1] * src[pl.ds(a * SUBLANES, tm), :]
    mu = jnp.mean(y, axis=-1, keepdims=True)
    yc = y - mu
    z = yc * lax.rsqrt(jnp.mean(yc * yc, axis=-1, keepdims=True) + NORM_EPS)
    z = z * lng_ref[...] + lnb_ref[...]
    z = (z * jax.nn.sigmoid(z)).astype(BF16)
    upd = (jnp.dot(attn_ref[0], wo_ref[:ATTN_WIDTH], preferred_element_type=F32)
           + jnp.dot(z, wo_ref[ATTN_WIDTH:], preferred_element_type=F32))
    o_ref[0] = h_ref[0] + upd


def _mix(h, attn, c, cw, cb, lng, lnb, wo):
    b, s, d = h.shape
    tm = ROW_TILE
    cwid = c.shape[-1]
    halo_blocks = tm // CONV_HALO
    return pl.pallas_call(
        _mix_kernel,
        grid=(b, s // tm),
        in_specs=[
            pl.BlockSpec((1, tm, d), lambda bi, i: (bi, i, 0)),
            pl.BlockSpec((1, tm, ATTN_WIDTH), lambda bi, i: (bi, i, 0)),
            pl.BlockSpec((1, tm, cwid), lambda bi, i: (bi, i, 0)),
            pl.BlockSpec((1, CONV_HALO, cwid),
                         lambda bi, i: (bi, jnp.maximum(i * halo_blocks - 1, 0), 0)),
            _const_spec(cw.shape),
            _const_spec((1, cwid)),
            _const_spec((1, cwid)),
            _const_spec((1, cwid)),
            _const_spec(wo.shape),
        ],
        out_specs=pl.BlockSpec((1, tm, d), lambda bi, i: (bi, i, 0)),
        out_shape=jax.ShapeDtypeStruct((b, s, d), F32),
        scratch_shapes=[pltpu.VMEM((CONV_HALO + tm, cwid), F32),
                        pltpu.VMEM((SUBLANES, CONV_HALO + tm - SUBLANES, cwid), F32)],
        compiler_params=pltpu.CompilerParams(
            dimension_semantics=("parallel", "parallel"), vmem_limit_bytes=VMEM_LIMIT_BYTES),
        name="mix",
    )(h, attn, c, c, cw, cb, lng, lnb, wo)


def _ffn_kernel(h_ref, halo_ref, p_ref, g2_ref, wa_ref, wb_ref, fwa_ref, fwb_ref, fba_ref,
                fbb_ref, wd_ref, pg_ref, wg_ref, bg_ref, wp_ref, o_ref):
    tm = h_ref.shape[1]
    first = pl.program_id(1) == 0
    h = h_ref[0]
    halo = halo_ref[0]
    halo = jnp.where(first, jnp.zeros_like(halo), halo)
    x = jnp.concatenate([halo, h], axis=0)
    xn = _rms_rows(x, g2_ref[...]).astype(BF16)

    def conv_up(w_ref, cw_ref, cb_ref):
        u = jnp.dot(xn, w_ref[...], preferred_element_type=F32)
        cw = cw_ref[...]
        y = cb_ref[...] + cw[2:3] * u[FFN_HALO:]
        y = y + cw[1:2] * u[FFN_HALO - 1:FFN_HALO - 1 + tm]
        y = y + cw[0:1] * u[FFN_HALO - 2:FFN_HALO - 2 + tm]
        return y

    ya = conv_up(wa_ref, fwa_ref, fba_ref)
    yb = conv_up(wb_ref, fwb_ref, fbb_ref)
    act = (ya * jax.nn.sigmoid(ya) * yb).astype(BF16)
    h = h + jnp.dot(act, wd_ref[...], preferred_element_type=F32)

    hn = _rms_rows(h, pg_ref[...]).astype(BF16)
    gate = jax.nn.sigmoid(jnp.dot(hn, wg_ref[...], preferred_element_type=F32) + bg_ref[...])
    ple = jnp.dot(p_ref[0].astype(BF16), wp_ref[...], preferred_element_type=F32)
    o_ref[0] = h + gate * ple


def _ffn(h, p, g2, wa, wb, fwa, fwb, fba, fbb, wd, pg, wg, bg, wp):
    b, s, d = h.shape
    tm = ROW_TILE
    halo_blocks = tm // FFN_HALO
    consts = (g2, wa, wb, fwa, fwb, fba, fbb, wd, pg, wg, bg, wp)
    return pl.pallas_call(
        _ffn_kernel,
        grid=(b, s // tm),
        in_specs=[
            pl.BlockSpec((1, tm, d), lambda bi, i: (bi, i, 0)),
            pl.BlockSpec((1, FFN_HALO, d),
                         lambda bi, i: (bi, jnp.maximum(i * halo_blocks - 1, 0), 0)),
            pl.BlockSpec((1, tm, p.shape[-1]), lambda bi, i: (bi, i, 0)),
        ] + [_const_spec(a.shape) for a in consts],
        out_specs=pl.BlockSpec((1, tm, d), lambda bi, i: (bi, i, 0)),
        out_shape=jax.ShapeDtypeStruct((b, s, d), F32),
        compiler_params=pltpu.CompilerParams(
            dimension_semantics=("parallel", "parallel"), vmem_limit_bytes=VMEM_LIMIT_BYTES),
        name="ffn",
    )(h, h, p, *consts)


def _pad_cols(a, width):
    return jnp.pad(a, ((0, 0), (0, width - a.shape[1])))


def kernel(x, p, positions, norm1_g, w_in, q_norm_g, k_norm_g, lam_q1, lam_k1, lam_q2, lam_k2,
           subln_g, conv_w, conv_b, conv_ln_g, conv_ln_b, w_out, norm2_g, w_up, ffn_conv_w,
           ffn_conv_b, w_down, ple_norm_g, w_ple_gate, b_ple_gate, w_ple):
    b, s, d = x.shape
    depth = w_in.shape[0]
    assert s % (Q_CHUNKS * ROW_TILE) == 0 and s % (IN_CHUNKS * ROW_TILE) == 0 and d % LANES == 0
    d_ff = w_down.shape[1]
    d_ff_pad = -(-d_ff // LANES) * LANES
    nt = s // ROW_TILE

    pos_f = positions.astype(F32).reshape(b, nt // IN_CHUNKS, 1, IN_CHUNKS * ROW_TILE)
    invf = (ROPE_THETA ** (-jnp.arange(0, ROT_DIM, 2, dtype=F32) / ROT_DIM)).reshape(HALF_ROT, 1)
    row = lambda v: v.reshape(1, -1).astype(F32)
    col = lambda v: v.reshape(-1, 1).astype(F32)

    h = x
    for i in range(depth):
        lam_init = 0.8 - 0.6 * math.exp(-0.3 * i)
        qkv_cols = 2 * Q_COLS + ATTN_WIDTH
        wqkv_t = _qkv_weight_t(w_in, i, qkv_cols)
        wglu = w_in[i][:, qkv_cols:].astype(BF16)
        q_t, k, v_t, c = _in_proj(h, pos_f, row(norm1_g[i]), wqkv_t, wglu,
                                  col(q_norm_g[i]), col(k_norm_g[i]), invf)

        lamv = jnp.stack([lam_q1[i], lam_k1[i], lam_q2[i], lam_k2[i]]).astype(F32)
        bound = (1.01 * QK_DIM ** 0.5 * LOG2_E) * (jnp.max(jnp.abs(q_norm_g[i]))
                                                   * jnp.max(jnp.abs(k_norm_g[i])))
        attn_args = (q_t, k, v_t, lamv, col(subln_g[i]))
        attn = lax.cond(bound <= SCORE_BOUND,
                        lambda *a: _attn(*a, lam_init, True),
                        lambda *a: _attn(*a, lam_init, False), *attn_args)

        wo = w_out[i].astype(BF16)
        h = _mix(h, attn, c, conv_w[i].astype(F32), row(conv_b[i]), row(conv_ln_g[i]),
                 row(conv_ln_b[i]), wo)

        wa = _pad_cols(w_up[i][:, :d_ff], d_ff_pad).astype(BF16)
        wb = _pad_cols(w_up[i][:, d_ff:], d_ff_pad).astype(BF16)
        fwa = _pad_cols(ffn_conv_w[i][:, :d_ff], d_ff_pad).astype(F32)
        fwb = _pad_cols(ffn_conv_w[i][:, d_ff:], d_ff_pad).astype(F32)
        fba = _pad_cols(row(ffn_conv_b[i][:d_ff]), d_ff_pad)
        fbb = _pad_cols(row(ffn_conv_b[i][d_ff:]), d_ff_pad)
        wd = jnp.pad(w_down[i], ((0, d_ff_pad - d_ff), (0, 0))).astype(BF16)
        h = _ffn(h, p[i], row(norm2_g[i]), wa, wb, fwa, fwb, fba, fbb, wd,
                 row(ple_norm_g[i]), w_ple_gate[i].astype(BF16), row(b_ple_gate[i]),
                 w_ple[i].astype(BF16))
    return h
```

```python
import functools
import math

import jax
import jax.numpy as jnp
from jax import lax
from jax.experimental import pallas as pl
from jax.experimental.pallas import tpu as pltpu

N_HEADS = 4
QK_DIM = 64
V_DIM = 2 * QK_DIM
ATTN_WIDTH = N_HEADS * V_DIM
Q_COLS = N_HEADS * 2 * QK_DIM
ROT_DIM = QK_DIM // 4
HALF_ROT = ROT_DIM // 2
ROPE_THETA = 500000.0
NORM_EPS = 1e-6
MASK_VALUE = -1e30
LOG2_E = math.log2(math.e)
V_AUG = V_DIM + 16

LANES = 128
SUBLANES = 8
VMEM_LIMIT_BYTES = 56 * 1024 * 1024

ROW_TILE = 512
IN_CHUNKS = 2
Q_CHUNKS = 2
STRIP = 512
SCORE_BOUND = 64.0
CONV_HALO = 32
FFN_HALO = 8

F32 = jnp.float32
BF16 = jnp.bfloat16


def _const_spec(shape):
    return pl.BlockSpec(shape, lambda *_: (0,) * len(shape), pipeline_mode=pl.Buffered(1))


def _rms_rows(x, g):
    ms = jnp.mean(x * x, axis=-1, keepdims=True)
    return x * lax.rsqrt(ms + NORM_EPS) * g


def _transpose_cast_kernel(w_ref, o_ref):
    o_ref[...] = w_ref[0].T.astype(BF16)


def _qkv_weight_t(w_in, layer, cols):
    _, d, _ = w_in.shape
    bn = ROW_TILE
    return pl.pallas_call(
        _transpose_cast_kernel,
        grid=(cols // bn,),
        in_specs=[pl.BlockSpec((1, d, bn), lambda j: (layer, 0, j))],
        out_specs=pl.BlockSpec((bn, d), lambda j: (j, 0)),
        out_shape=jax.ShapeDtypeStruct((cols, d), BF16),
        compiler_params=pltpu.CompilerParams(
            dimension_semantics=("parallel",), vmem_limit_bytes=VMEM_LIMIT_BYTES),
        name="qkv_weight_t",
    )(w_in)


def _in_proj_kernel(h_ref, pos_ref, g1_ref, wqkv_t_ref, wglu_ref, qg_ref, kg_ref, invf_ref,
                    q_t_ref, k_ref, v_t_ref, c_ref, *, conv_width):
    hn = _rms_rows(h_ref[0], g1_ref[...]).astype(BF16)
    u_t = lax.dot_general(wqkv_t_ref[...], hn, (((1,), (1,)), ((), ())),
                          preferred_element_type=F32)
    glu = jnp.dot(hn, wglu_ref[...], preferred_element_type=F32)
    c_ref[0] = glu[:, :conv_width] * jax.nn.sigmoid(glu[:, conv_width:])

    ang = invf_ref[...] * pos_ref[0, 0]
    cos, sin = jnp.cos(ang), jnp.sin(ang)

    def norm_rope(x_t, g):
        segs = []
        for s in range(Q_COLS // QK_DIM):
            seg = x_t[s * QK_DIM:(s + 1) * QK_DIM]
            ms = jnp.mean(seg * seg, axis=0, keepdims=True)
            y = seg * lax.rsqrt(ms + NORM_EPS) * g
            x1, x2 = y[:HALF_ROT], y[HALF_ROT:ROT_DIM]
            segs += [x1 * cos - x2 * sin, x2 * cos + x1 * sin, y[ROT_DIM:]]
        return jnp.concatenate(segs, axis=0)

    q_t = norm_rope(u_t[:Q_COLS], qg_ref[...]) * (QK_DIM ** -0.5 * LOG2_E)
    k_t = norm_rope(u_t[Q_COLS:2 * Q_COLS], kg_ref[...])
    k_ref[0] = k_t.T.astype(BF16)
    t = q_t_ref.shape[-1]
    ones = jnp.ones((V_AUG - V_DIM, t), BF16)
    for ch in range(IN_CHUNKS):
        cols = slice(ch * t, (ch + 1) * t)
        q_t_ref[0, ch] = q_t[:, cols].astype(BF16)
        for hd in range(N_HEADS):
            v_rows = u_t[2 * Q_COLS + hd * V_DIM:2 * Q_COLS + (hd + 1) * V_DIM, cols]
            v_t_ref[0, ch, hd, :V_DIM] = v_rows.astype(BF16)
            v_t_ref[0, ch, hd, V_DIM:] = ones


def _in_proj(h, pos_f, g1, wqkv_t, wglu, qg, kg, invf):
    b, s, d = h.shape
    t = ROW_TILE
    tm = IN_CHUNKS * t
    nt = s // t
    conv_width = wglu.shape[1] // 2
    qkv_rows = wqkv_t.shape[0]
    return pl.pallas_call(
        functools.partial(_in_proj_kernel, conv_width=conv_width),
        grid=(b, s // tm),
        in_specs=[
            pl.BlockSpec((1, tm, d), lambda bi, i: (bi, i, 0)),
            pl.BlockSpec((1, 1, 1, tm), lambda bi, i: (bi, i, 0, 0)),
            _const_spec((1, d)),
            _const_spec((qkv_rows, d)),
            _const_spec((d, 2 * conv_width)),
            _const_spec((QK_DIM, 1)),
            _const_spec((QK_DIM, 1)),
            _const_spec((HALF_ROT, 1)),
        ],
        out_specs=[
            pl.BlockSpec((1, IN_CHUNKS, Q_COLS, t), lambda bi, i: (bi, i, 0, 0)),
            pl.BlockSpec((1, tm, Q_COLS), lambda bi, i: (bi, i, 0)),
            pl.BlockSpec((1, IN_CHUNKS, N_HEADS, V_AUG, t), lambda bi, i: (bi, i, 0, 0, 0)),
            pl.BlockSpec((1, tm, conv_width), lambda bi, i: (bi, i, 0)),
        ],
        out_shape=[
            jax.ShapeDtypeStruct((b, nt, Q_COLS, t), BF16),
            jax.ShapeDtypeStruct((b, s, Q_COLS), BF16),
            jax.ShapeDtypeStruct((b, nt, N_HEADS, V_AUG, t), BF16),
            jax.ShapeDtypeStruct((b, s, conv_width), F32),
        ],
        compiler_params=pltpu.CompilerParams(
            dimension_semantics=("parallel", "parallel"), vmem_limit_bytes=VMEM_LIMIT_BYTES),
        name="in_proj",
    )(h, pos_f, g1, wqkv_t, wglu, qg, kg, invf)


def _attn_kernel(q_t_ref, k_ref, v_t_ref, lamv_ref, subg_ref, o_ref, acc_ref, sa_ref, sb_ref,
                 pa_ref, pb_ref, *, lam_init):
    t = k_ref.shape[1] // v_t_ref.shape[1]
    tq = Q_CHUNKS * t
    qi = pl.program_id(2)
    q_t = jnp.concatenate([q_t_ref[0, c] for c in range(Q_CHUNKS)], axis=1)
    zero_half = jnp.zeros((QK_DIM, tq), BF16)
    q_maps = (jnp.concatenate([q_t[:QK_DIM], zero_half], axis=0),
              jnp.concatenate([zero_half, q_t[QK_DIM:]], axis=0))

    def keys(j):
        return k_ref[0, pl.ds(pl.multiple_of(j * t, t), t), :]

    units = [(idx, slice(c * STRIP, (c + 1) * STRIP)) for c in range(tq // STRIP)
             for idx in range(2)]

    def score_unit(kc, s_ref, idx, cols):
        s = jnp.dot(kc, q_maps[idx][:, cols], preferred_element_type=F32)
        s_ref[idx, :, cols] = s
        return jnp.max(s, axis=0, keepdims=True)

    def numerator_unit(s_ref, p_ref, idx, cols, m_new):
        for r in range(2):
            rows = slice(r * (t // 2), (r + 1) * (t // 2))
            p_ref[idx, rows, cols] = jnp.exp2((s_ref[idx, rows, cols] - m_new).astype(BF16))

    def accumulate_unit(vc, p_ref, idx, cols, alpha):
        acc_ref[idx, :, cols] = alpha * acc_ref[idx, :, cols] + jnp.dot(
            vc, p_ref[idx, :, cols], preferred_element_type=F32)

    def stage(j_acc, p_acc_ref, alphas, s_cur_ref, p_cur_ref, ms, cmax, j_next, s_next_ref):
        vc = v_t_ref[0, j_acc, 0]
        kc = keys(j_next)
        new_ms, new_alphas, new_cmax = [], [], []
        for u, (idx, cols) in enumerate(units):
            new_cmax.append(score_unit(kc, s_next_ref, idx, cols))
            m_new = jnp.maximum(ms[u], cmax[u])
            numerator_unit(s_cur_ref, p_cur_ref, idx, cols, m_new)
            new_ms.append(m_new)
            new_alphas.append(jnp.exp2(ms[u] - m_new))
            accumulate_unit(vc, p_acc_ref, idx, cols, alphas[u])
        return tuple(new_ms), tuple(new_alphas), tuple(new_cmax)

    d0 = Q_CHUNKS * qi
    tri = (lax.broadcasted_iota(jnp.int32, (t, t), 0)
           <= lax.broadcasted_iota(jnp.int32, (t, t), 1))
    acc_ref[...] = jnp.zeros_like(acc_ref)
    kc_first = keys(0)
    kc = keys(d0 + 1)
    ms, alphas, cmax = [], [], []
    m_lo = jnp.full((1, STRIP), MASK_VALUE, F32)
    one = jnp.ones((1, STRIP), F32)
    for idx, cols in units:
        cmax.append(score_unit(kc_first, sa_ref, idx, cols))
        if cols.stop <= t:
            pb_ref[idx, :, cols] = jnp.zeros((t, STRIP), BF16)
            ms.append(m_lo)
        else:
            s = jnp.dot(kc, q_maps[idx][:, cols], preferred_element_type=F32)
            s = jnp.where(tri[:, cols.start - t:cols.stop - t], s, MASK_VALUE)
            m = jnp.max(s, axis=0, keepdims=True)
            pb_ref[idx, :, cols] = jnp.exp2((s - m).astype(BF16))
            ms.append(m)
        alphas.append(one)
    carry = (tuple(ms), tuple(alphas), tuple(cmax))

    def body(jj, carry):
        ms, alphas, cmax = carry
        c0 = Q_CHUNKS * jj
        ms, alphas, cmax = stage(jnp.where(jj == 0, d0 + 1, c0 - 1), pb_ref, alphas,
                                 sa_ref, pa_ref, ms, cmax, c0 + 1, sb_ref)
        return stage(c0, pa_ref, alphas, sb_ref, pb_ref, ms, cmax, c0 + 2, sa_ref)

    ms, alphas, _ = lax.fori_loop(0, qi, body, carry)

    vc_pending = v_t_ref[0, jnp.where(qi == 0, d0 + 1, d0 - 1), 0]
    vc = v_t_ref[0, d0, 0]
    for u, (idx, cols) in enumerate(units):
        accumulate_unit(vc_pending, pb_ref, idx, cols, alphas[u])
        s = sa_ref[idx, :, cols]
        if cols.stop <= t:
            s = jnp.where(tri[:, cols], s, MASK_VALUE)
        m_new = jnp.maximum(ms[u], jnp.max(s, axis=0, keepdims=True))
        pa_ref[idx, :, cols] = jnp.exp2((s - m_new).astype(BF16))
        accumulate_unit(vc, pa_ref, idx, cols, jnp.exp2(ms[u] - m_new))

    o_ref[0] = _attn_finalize(acc_ref, lamv_ref, subg_ref, lam_init)


def _attn_bounded_kernel(q_t_ref, k_ref, v_t_ref, lamv_ref, subg_ref, o_ref, acc_ref, pa_ref,
                         pb_ref, l_ref, pc_ref, *, lam_init):
    t = k_ref.shape[1] // v_t_ref.shape[1]
    tq = Q_CHUNKS * t
    qi = pl.program_id(2)
    q_t = jnp.concatenate([q_t_ref[0, c] for c in range(Q_CHUNKS)], axis=1)
    zero_half = jnp.zeros((QK_DIM, tq), BF16)
    q_maps = (jnp.concatenate([q_t[:QK_DIM], zero_half], axis=0),
              jnp.concatenate([zero_half, q_t[QK_DIM:]], axis=0))

    def keys(j):
        return k_ref[0, pl.ds(pl.multiple_of(j * t, t), t), :]

    units = [(idx, slice(c * STRIP, (c + 1) * STRIP)) for c in range(tq // STRIP)
             for idx in range(2)]
    tri = (lax.broadcasted_iota(jnp.int32, (t, t), 0)
           <= lax.broadcasted_iota(jnp.int32, (t, t), 1))

    def numerator_unit(kc, p_ref, idx, cols, mask=None, l_cols=None):
        l_cols = cols if l_cols is None else l_cols
        s = jnp.dot(kc, q_maps[idx][:, l_cols], preferred_element_type=F32)
        if mask is not None:
            s = jnp.where(mask, s, MASK_VALUE)
        p = jnp.exp2(s)
        p_ref[idx, :, cols] = p.astype(BF16)
        l_ref[idx, :, l_cols] += jnp.sum(p.reshape(t // SUBLANES, SUBLANES, STRIP), axis=0)

    def accumulate_unit(vc, p_ref, idx, cols):
        acc_ref[idx, :V_DIM, cols] += jnp.dot(vc[:V_DIM], p_ref[idx, :, cols],
                                              preferred_element_type=F32)

    def stage(j_acc, p_acc_ref, j_new, p_new_ref):
        vc = v_t_ref[0, j_acc, 0]
        kc = keys(j_new)
        for idx, cols in units:
            numerator_unit(kc, p_new_ref, idx, cols)
            accumulate_unit(vc, p_acc_ref, idx, cols)

    d0 = Q_CHUNKS * qi
    acc_ref[...] = jnp.zeros_like(acc_ref)
    l_ref[...] = jnp.zeros_like(l_ref)
    kc_hi, kc_lo = keys(d0 + 1), keys(d0)
    vc_hi = v_t_ref[0, d0 + 1, 0]
    for idx, cols in units:
        if cols.stop <= t:
            numerator_unit(kc_lo, pa_ref, idx, cols, tri[:, cols])
        else:
            numerator_unit(kc_lo, pa_ref, idx, cols)
            numerator_unit(kc_hi, pc_ref, idx, slice(cols.start - t, cols.stop - t),
                           tri[:, cols.start - t:cols.stop - t], l_cols=cols)

    def chunk_pair(jj):
        c0 = Q_CHUNKS * jj
        stage(jnp.where(jj == 0, d0, c0 - 1), pa_ref, c0, pb_ref)
        stage(c0, pb_ref, c0 + 1, pa_ref)

    def body(jj, carry):
        chunk_pair(2 * jj)
        chunk_pair(2 * jj + 1)
        return carry

    lax.fori_loop(0, qi // 2, body, 0)

    @pl.when(qi % 2 == 1)
    def _():
        chunk_pair(qi - 1)

    vc = v_t_ref[0, jnp.where(qi == 0, d0, d0 - 1), 0]
    for idx, cols in units:
        accumulate_unit(vc, pa_ref, idx, cols)
        if cols.start >= t:
            acc_ref[idx, :V_DIM, cols] += jnp.dot(
                vc_hi[:V_DIM], pc_ref[idx, :, cols.start - t:cols.stop - t],
                preferred_element_type=F32)

    for idx in range(2):
        acc_ref[idx, V_DIM:V_DIM + 1] = jnp.sum(l_ref[idx], axis=0, keepdims=True)
    o_ref[0] = _attn_finalize(acc_ref, lamv_ref, subg_ref, lam_init)


def _attn_finalize(acc_ref, lamv_ref, subg_ref, lam_init):
    lamv = lamv_ref[...]
    lam = (jnp.exp(jnp.sum(lamv[0:1] * lamv[1:2], axis=-1, keepdims=True))
           - jnp.exp(jnp.sum(lamv[2:3] * lamv[3:4], axis=-1, keepdims=True)) + lam_init)
    o1 = acc_ref[0, :V_DIM] * (1.0 / acc_ref[0, V_DIM:V_DIM + 1])
    o2 = acc_ref[1, :V_DIM] * (1.0 / acc_ref[1, V_DIM:V_DIM + 1])
    o = o1 - lam * o2
    ms_o = jnp.mean(o * o, axis=0, keepdims=True)
    y = o * lax.rsqrt(ms_o + NORM_EPS) * subg_ref[...] * (1.0 - lam_init)
    return y.T.astype(BF16)


def _attn(q_t, k, v_t, lamv, subg, lam_init, bounded):
    b, nt, _, t = q_t.shape
    s = k.shape[1]
    tq = Q_CHUNKS * t
    p_bufs = [pltpu.VMEM((2, t, tq), BF16), pltpu.VMEM((2, t, tq), BF16)]
    s_bufs = [] if bounded else [pltpu.VMEM((2, t, tq), F32), pltpu.VMEM((2, t, tq), F32)]
    body = _attn_bounded_kernel if bounded else _attn_kernel
    return pl.pallas_call(
        functools.partial(body, lam_init=lam_init),
        grid=(b, N_HEADS, s // tq),
        in_specs=[
            pl.BlockSpec((1, Q_CHUNKS, 2 * QK_DIM, t), lambda bi, hi, qi: (bi, qi, hi, 0)),
            pl.BlockSpec((1, s, 2 * QK_DIM), lambda bi, hi, qi: (bi, 0, hi)),
            pl.BlockSpec((1, nt, 1, V_AUG, t), lambda bi, hi, qi: (bi, 0, hi, 0, 0)),
            _const_spec((4, QK_DIM)),
            _const_spec((V_DIM, 1)),
        ],
        out_specs=pl.BlockSpec((1, tq, V_DIM), lambda bi, hi, qi: (bi, qi, hi)),
        out_shape=jax.ShapeDtypeStruct((b, s, ATTN_WIDTH), BF16),
        scratch_shapes=([pltpu.VMEM((2, V_AUG, tq), F32)] + s_bufs + p_bufs
                        + ([pltpu.VMEM((2, SUBLANES, tq), F32),
                            pltpu.VMEM((2, t, tq - t), BF16)] if bounded else [])),
        compiler_params=pltpu.CompilerParams(
            dimension_semantics=("parallel", "parallel", "arbitrary"),
            vmem_limit_bytes=VMEM_LIMIT_BYTES),
        name="attn_bounded" if bounded else "attn",
    )(q_t, k, v_t, lamv, subg)


def _mix_kernel(h_ref, attn_ref, c_ref, halo_ref, cw_ref, cb_ref, lng_ref, lnb_ref,
                wo_ref, o_ref, buf_ref, shift_ref):
    tm = c_ref.shape[1]
    taps = cw_ref.shape[0]
    first = pl.program_id(1) == 0
    halo = halo_ref[0]
    buf_ref[:CONV_HALO] = jnp.where(first, jnp.zeros_like(halo), halo)
    buf_ref[CONV_HALO:] = c_ref[0]
    base = CONV_HALO - (taps - 1)
    span = shift_ref.shape[1]
    for r in range(1, SUBLANES):
        shift_ref[r] = buf_ref[pl.ds(r, span), :]
    cw = cw_ref[...]
    y = cb_ref[...]
    for j in range(taps):
        r, a = (base + j) % SUBLANES, (base + j) // SUBLANES
        src = buf_ref if r == 0 else shift_ref.at[r]
        y = y + cw[j:j + 1] * src[pl.ds(a * SUBLANES, tm), :]
    mu = jnp.mean(y, axis=-1, keepdims=True)
    yc = y - mu
    z = yc * lax.rsqrt(jnp.mean(yc * yc, axis=-1, keepdims=True) + NORM_EPS)
    z = z * lng_ref[...] + lnb_ref[...]
    z = (z * jax.nn.sigmoid(z)).astype(BF16)
    upd = (jnp.dot(attn_ref[0], wo_ref[:ATTN_WIDTH], preferred_element_type=F32)
           + jnp.dot(z, wo_ref[ATTN_WIDTH:], preferred_element_type=F32))
    o_ref[0] = h_ref[0] + upd


def _mix(h, attn, c, cw, cb, lng, lnb, wo):
    b, s, d = h.shape
    tm = ROW_TILE
    cwid = c.shape[-1]
    halo_blocks = tm // CONV_HALO
    return pl.pallas_call(
        _mix_kernel,
        grid=(b, s // tm),
        in_specs=[
            pl.BlockSpec((1, tm, d), lambda bi, i: (bi, i, 0)),
            pl.BlockSpec((1, tm, ATTN_WIDTH), lambda bi, i: (bi, i, 0)),
            pl.BlockSpec((1, tm, cwid), lambda bi, i: (bi, i, 0)),
            pl.BlockSpec((1, CONV_HALO, cwid),
                         lambda bi, i: (bi, jnp.maximum(i * halo_blocks - 1, 0), 0)),
            _const_spec(cw.shape),
            _const_spec((1, cwid)),
            _const_spec((1, cwid)),
            _const_spec((1, cwid)),
            _const_spec(wo.shape),
        ],
        out_specs=pl.BlockSpec((1, tm, d), lambda bi, i: (bi, i, 0)),
        out_shape=jax.ShapeDtypeStruct((b, s, d), F32),
        scratch_shapes=[pltpu.VMEM((CONV_HALO + tm, cwid), F32),
                        pltpu.VMEM((SUBLANES, CONV_HALO + tm - SUBLANES, cwid), F32)],
        compiler_params=pltpu.CompilerParams(
            dimension_semantics=("parallel", "parallel"), vmem_limit_bytes=VMEM_LIMIT_BYTES),
        name="mix",
    )(h, attn, c, c, cw, cb, lng, lnb, wo)


def _ffn_kernel(h_ref, halo_ref, p_ref, g2_ref, wa_ref, wb_ref, fwa_ref, fwb_ref, fba_ref,
                fbb_ref, wd_ref, pg_ref, wg_ref, bg_ref, wp_ref, o_ref):
    tm = h_ref.shape[1]
    first = pl.program_id(1) == 0
    h = h_ref[0]
    halo = halo_ref[0]
    halo = jnp.where(first, jnp.zeros_like(halo), halo)
    x = jnp.concatenate([halo, h], axis=0)
    xn = _rms_rows(x, g2_ref[...]).astype(BF16)

    def conv_up(w_ref, cw_ref, cb_ref):
        u = jnp.dot(xn, w_ref[...], preferred_element_type=F32)
        cw = cw_ref[...]
        y = cb_ref[...] + cw[2:3] * u[FFN_HALO:]
        y = y + cw[1:2] * u[FFN_HALO - 1:FFN_HALO - 1 + tm]
        y = y + cw[0:1] * u[FFN_HALO - 2:FFN_HALO - 2 + tm]
        return y

    ya = conv_up(wa_ref, fwa_ref, fba_ref)
    yb = conv_up(wb_ref, fwb_ref, fbb_ref)
    act = (ya * jax.nn.sigmoid(ya) * yb).astype(BF16)
    h = h + jnp.dot(act, wd_ref[...], preferred_element_type=F32)

    hn = _rms_rows(h, pg_ref[...]).astype(BF16)
    gate = jax.nn.sigmoid(jnp.dot(hn, wg_ref[...], preferred_element_type=F32) + bg_ref[...])
    ple = jnp.dot(p_ref[0, 0].astype(BF16), wp_ref[...], preferred_element_type=F32)
    o_ref[0] = h + gate * ple


def _ffn(h, p, layer, g2, wa, wb, fwa, fwb, fba, fbb, wd, pg, wg, bg, wp):
    b, s, d = h.shape
    tm = ROW_TILE
    halo_blocks = tm // FFN_HALO
    consts = (g2, wa, wb, fwa, fwb, fba, fbb, wd, pg, wg, bg, wp)
    return pl.pallas_call(
        _ffn_kernel,
        grid=(b, s // tm),
        in_specs=[
            pl.BlockSpec((1, tm, d), lambda bi, i: (bi, i, 0)),
            pl.BlockSpec((1, FFN_HALO, d),
                         lambda bi, i: (bi, jnp.maximum(i * halo_blocks - 1, 0), 0)),
            pl.BlockSpec((1, 1, tm, p.shape[-1]), lambda bi, i: (layer, bi, i, 0)),
        ] + [_const_spec(a.shape) for a in consts],
        out_specs=pl.BlockSpec((1, tm, d), lambda bi, i: (bi, i, 0)),
        out_shape=jax.ShapeDtypeStruct((b, s, d), F32),
        compiler_params=pltpu.CompilerParams(
            dimension_semantics=("parallel", "parallel"), vmem_limit_bytes=VMEM_LIMIT_BYTES),
        name="ffn",
    )(h, h, p, *consts)


def _pad_cols(a, width):
    return jnp.pad(a, ((0, 0), (0, width - a.shape[1])))


def kernel(x, p, positions, norm1_g, w_in, q_norm_g, k_norm_g, lam_q1, lam_k1, lam_q2, lam_k2,
           subln_g, conv_w, conv_b, conv_ln_g, conv_ln_b, w_out, norm2_g, w_up, ffn_conv_w,
           ffn_conv_b, w_down, ple_norm_g, w_ple_gate, b_ple_gate, w_ple):
    b, s, d = x.shape
    depth = w_in.shape[0]
    assert s % (Q_CHUNKS * ROW_TILE) == 0 and s % (IN_CHUNKS * ROW_TILE) == 0 and d % LANES == 0
    d_ff = w_down.shape[1]
    d_ff_pad = -(-d_ff // LANES) * LANES
    nt = s // ROW_TILE

    pos_f = positions.astype(F32).reshape(b, nt // IN_CHUNKS, 1, IN_CHUNKS * ROW_TILE)
    invf = (ROPE_THETA ** (-jnp.arange(0, ROT_DIM, 2, dtype=F32) / ROT_DIM)).reshape(HALF_ROT, 1)
    row = lambda v: v.reshape(1, -1).astype(F32)
    col = lambda v: v.reshape(-1, 1).astype(F32)

    h = x
    for i in range(depth):
        lam_init = 0.8 - 0.6 * math.exp(-0.3 * i)
        qkv_cols = 2 * Q_COLS + ATTN_WIDTH
        wqkv_t = _qkv_weight_t(w_in, i, qkv_cols)
        wglu = w_in[i][:, qkv_cols:].astype(BF16)
        q_t, k, v_t, c = _in_proj(h, pos_f, row(norm1_g[i]), wqkv_t, wglu,
                                  col(q_norm_g[i]), col(k_norm_g[i]), invf)

        lamv = jnp.stack([lam_q1[i], lam_k1[i], lam_q2[i], lam_k2[i]]).astype(F32)
        bound = (1.01 * QK_DIM ** 0.5 * LOG2_E) * (jnp.max(jnp.abs(q_norm_g[i]))
                                                   * jnp.max(jnp.abs(k_norm_g[i])))
        attn_args = (q_t, k, v_t, lamv, col(subln_g[i]))
        attn = lax.cond(bound <= SCORE_BOUND,
                        lambda *a: _attn(*a, lam_init, True),
                        lambda *a: _attn(*a, lam_init, False), *attn_args)

        wo = w_out[i].astype(BF16)
        h = _mix(h, attn, c, conv_w[i].astype(F32), row(conv_b[i]), row(conv_ln_g[i]),
                 row(conv_ln_b[i]), wo)

        wa = _pad_cols(w_up[i][:, :d_ff], d_ff_pad).astype(BF16)
        wb = _pad_cols(w_up[i][:, d_ff:], d_ff_pad).astype(BF16)
        fwa = _pad_cols(ffn_conv_w[i][:, :d_ff], d_ff_pad).astype(F32)
        fwb = _pad_cols(ffn_conv_w[i][:, d_ff:], d_ff_pad).astype(F32)
        fba = _pad_cols(row(ffn_conv_b[i][:d_ff]), d_ff_pad)
        fbb = _pad_cols(row(ffn_conv_b[i][d_ff:]), d_ff_pad)
        wd = jnp.pad(w_down[i], ((0, d_ff_pad - d_ff), (0, 0))).astype(BF16)
        h = _ffn(h, p, i, row(norm2_g[i]), wa, wb, fwa, fwb, fba, fbb, wd,
                 row(ple_norm_g[i]), w_ple_gate[i].astype(BF16), row(b_ple_gate[i]),
                 w_ple[i].astype(BF16))
    return h
```
